```python
import jax, jax.numpy as jnp
from jax import lax
import numpy as np

D_MODEL = 1024
BATCH = 32
SEQ = 2048
DEPTH = 1

POOL_WIDTH = 256
POOL_GROUPS = 4
POOL_GROUP_DIM = POOL_WIDTH // POOL_GROUPS
POOL_WINDOWS = (2, 4, 8, 16)
GDN_HEADS = 6
GDN_HEAD_DIM = 128
GDN_WIDTH = GDN_HEADS * GDN_HEAD_DIM
D_MIX = POOL_WIDTH + GDN_WIDTH
CONV_WIDTH = 4
CHUNK = 64
D_FF = -(-8 * D_MODEL // (3 * 256)) * 256
IN_COLS = POOL_WIDTH + 4 * GDN_WIDTH + 2 * GDN_HEADS
N_MOD = 6
EPS = 1e-6

kernel_name = 'hybrid_pool_gdn_adaln_block'


def rms_norm(x, gain=None):
    xf = x.astype(jnp.float32)
    y = xf * lax.rsqrt(jnp.mean(xf * xf, axis=-1, keepdims=True) + EPS)
    if gain is not None:
        y = y * gain.astype(jnp.float32)
    return y


def l2_normalize(x):
    return x * lax.rsqrt(jnp.sum(x * x, axis=-1, keepdims=True) + EPS)


def modulate(h, shift, scale):
    return h * (1.0 + scale) + shift


def pool_mixer(u, w_pool, pool_scale):
    B, S, _ = u.shape
    uf = u.astype(jnp.float32).reshape(B, S, POOL_GROUPS, POOL_GROUP_DIM)
    csum = jnp.cumsum(uf, axis=1)
    pos = jnp.arange(1, S + 1, dtype=jnp.float32)
    outs = []
    for gi, w in enumerate(POOL_WINDOWS):
        cs = csum[:, :, gi]
        lag = jnp.pad(cs[:, :S - w], ((0, 0), (w, 0), (0, 0)))
        mean = (cs - lag) / jnp.minimum(pos, w)[None, :, None]
        outs.append(mean - uf[:, :, gi])
    pooled = jnp.stack(outs, axis=2)
    mixed = jnp.einsum('bsgc,gcd->bsgd', pooled, w_pool.astype(jnp.float32))
    mixed = rms_norm(mixed.reshape(B, S, POOL_WIDTH))
    return mixed * pool_scale.astype(jnp.float32)


def causal_depthwise_conv(x, w):
    C = x.shape[-1]
    return lax.conv_general_dilated(
        x, w[:, None, :], window_strides=(1,), padding=[(CONV_WIDTH - 1, 0)],
        dimension_numbers=('NWC', 'WIO', 'NWC'), feature_group_count=C)


def chunk_gated_delta_rule(q, k, v, g, beta):
    B, S, H, Dk = q.shape
    Dv = v.shape[-1]
    N = S // CHUNK
    to_chunks = lambda t: t.reshape(B, N, CHUNK, H, -1).transpose(0, 3, 1, 2, 4)
    q, k, v = to_chunks(q), to_chunks(k), to_chunks(v)
    g = jnp.cumsum(g.reshape(B, N, CHUNK, H).transpose(0, 3, 1, 2), axis=-1)
    beta = beta.reshape(B, N, CHUNK, H).transpose(0, 3, 1, 2)
    idx = jnp.arange(CHUNK)
    causal = idx[:, None] >= idx[None, :]
    strict = idx[:, None] > idx[None, :]
    gdiff = g[..., :, None] - g[..., None, :]
    decay = jnp.where(causal, jnp.exp(jnp.where(causal, gdiff, 0.0)), 0.0)
    kb = k * beta[..., None]
    lower = jnp.where(strict, jnp.einsum('bhnid,bhnjd->bhnij', kb, k) * decay, 0.0)
    eye = jnp.eye(CHUNK, dtype=jnp.float32)
    rhs = jnp.concatenate([v * beta[..., None], kb * jnp.exp(g)[..., None]], axis=-1)
    sol = lax.linalg.triangular_solve(eye + lower, rhs, left_side=True, lower=True,
                                      unit_diagonal=True)
    u, w = sol[..., :Dv], sol[..., Dv:]
    qk = jnp.einsum('bhnid,bhnjd->bhnij', q, k) * decay
    xs = tuple(jnp.moveaxis(t, 2, 0) for t in (q, k, u, w, g, qk))

    def step(state, inp):
        q_c, k_c, u_c, w_c, g_c, qk_c = inp
        v_new = u_c - jnp.einsum('bhik,bhkv->bhiv', w_c, state)
        o_c = (jnp.einsum('bhik,bhkv->bhiv', q_c * jnp.exp(g_c)[..., None], state)
               + jnp.einsum('bhij,bhjv->bhiv', qk_c, v_new))
        g_last = g_c[..., -1:]
        state = (state * jnp.exp(g_last)[..., None]
                 + jnp.einsum('bhik,bhiv->bhkv', k_c * jnp.exp(g_last - g_c)[..., None], v_new))
        return state, o_c

    state0 = jnp.zeros((B, H, Dk, Dv), jnp.float32)
    _, o = lax.scan(step, state0, xs)
    return o.transpose(1, 0, 3, 2, 4).reshape(B, S, H, Dv)


def gated_deltanet_mixer(qkv, z, b, a, conv_w, a_log, dt_bias, o_gain):
    B, S, _ = qkv.shape
    qkv = jax.nn.silu(causal_depthwise_conv(qkv.astype(jnp.float32), conv_w.astype(jnp.float32)))
    q, k, v = jnp.split(qkv, 3, axis=-1)
    q = l2_normalize(q.reshape(B, S, GDN_HEADS, GDN_HEAD_DIM)) * (GDN_HEAD_DIM ** -0.5)
    k = l2_normalize(k.reshape(B, S, GDN_HEADS, GDN_HEAD_DIM))
    v = v.reshape(B, S, GDN_HEADS, GDN_HEAD_DIM)
    beta = jax.nn.sigmoid(b.astype(jnp.float32))
    g = -jnp.exp(a_log.astype(jnp.float32)) * jax.nn.softplus(a.astype(jnp.float32) + dt_bias.astype(jnp.float32))
    o = chunk_gated_delta_rule(q, k, v, g, beta)
    o = rms_norm(o, o_gain) * jax.nn.silu(z.reshape(B, S, GDN_HEADS, GDN_HEAD_DIM).astype(jnp.float32))
    return o.reshape(B, S, GDN_WIDTH)


def setup_inputs(seed: int = 0) -> dict:
    key = jax.random.key(seed)
    ks = jax.random.split(key, 20)
    f32 = jnp.float32
    nrm = lambda k, shape, s: jax.random.normal(k, shape, f32) * s
    dt = jnp.exp(jax.random.uniform(ks[7], (DEPTH, GDN_HEADS), f32, np.log(1e-3), np.log(1e-1)))
    return {
        'x': nrm(ks[0], (BATCH, SEQ, D_MODEL), 1.0),
        'c': nrm(ks[1], (BATCH, D_MODEL), 1.0),
        'w_ada': nrm(ks[2], (DEPTH, D_MODEL, N_MOD * D_MODEL), 0.5 * D_MODEL ** -0.5),
        'b_ada': nrm(ks[3], (DEPTH, N_MOD * D_MODEL), 0.01),
        'norm_mix_gain': 1.0 + nrm(ks[4], (DEPTH, D_MODEL), 0.1),
        'w_in': nrm(ks[5], (DEPTH, D_MODEL, IN_COLS), D_MODEL ** -0.5),
        'conv_w': nrm(ks[6], (DEPTH, CONV_WIDTH, 3 * GDN_WIDTH), CONV_WIDTH ** -0.5),
        'a_log': jnp.log(jax.random.uniform(ks[8], (DEPTH, GDN_HEADS), f32, 1.0, 16.0)),
        'dt_bias': dt + jnp.log(-jnp.expm1(-dt)),
        'gdn_norm_gain': 1.0 + nrm(ks[9], (DEPTH, GDN_HEAD_DIM), 0.1),
        'w_pool': nrm(ks[10], (DEPTH, POOL_GROUPS, POOL_GROUP_DIM, POOL_GROUP_DIM), POOL_GROUP_DIM ** -0.5),
        'pool_scale': 1.0 + nrm(ks[11], (DEPTH, POOL_WIDTH), 0.1),
        'w_out': nrm(ks[12], (DEPTH, D_MIX, D_MODEL), D_MIX ** -0.5),
        'norm_ffn_gain': 1.0 + nrm(ks[13], (DEPTH, D_MODEL), 0.1),
        'w_gate_up': nrm(ks[14], (DEPTH, D_MODEL, 2 * D_FF), D_MODEL ** -0.5),
        'w_down': nrm(ks[15], (DEPTH, D_FF, D_MODEL), D_FF ** -0.5),
        'norm_final_gain': 1.0 + nrm(ks[16], (D_MODEL,), 0.1),
    }


def reference(x, c, w_ada, b_ada, norm_mix_gain, w_in, conv_w, a_log, dt_bias,
              gdn_norm_gain, w_pool, pool_scale, w_out, norm_ffn_gain, w_gate_up,
              w_down, norm_final_gain):
    out_dtype = x.dtype
    h = x.astype(jnp.float32)
    cond = jax.nn.silu(c.astype(jnp.float32))
    splits = [POOL_WIDTH, POOL_WIDTH + 3 * GDN_WIDTH, POOL_WIDTH + 4 * GDN_WIDTH,
              POOL_WIDTH + 4 * GDN_WIDTH + GDN_HEADS]
    for l in range(DEPTH):
        mod = cond @ w_ada[l] + b_ada[l]
        sh_m, sc_m, gt_m, sh_f, sc_f, gt_f = jnp.split(mod[:, None, :], N_MOD, axis=-1)
        hn = modulate(rms_norm(h, norm_mix_gain[l]), sh_m, sc_m)
        proj = hn @ w_in[l]
        u, qkv, z, b, a = jnp.split(proj, splits, axis=-1)
        y_pool = pool_mixer(u, w_pool[l], pool_scale[l])
        y_gdn = gated_deltanet_mixer(qkv, z, b, a, conv_w[l], a_log[l], dt_bias[l],
                                     gdn_norm_gain[l])
        mix = jnp.concatenate([y_pool, y_gdn], axis=-1) @ w_out[l]
        h = h + gt_m * mix
        hn = modulate(rms_norm(h, norm_ffn_gain[l]), sh_f, sc_f)
        gate, up = jnp.split(hn @ w_gate_up[l], 2, axis=-1)
        h = h + gt_f * ((jax.nn.silu(gate) * up) @ w_down[l])
    return rms_norm(h, norm_final_gain).astype(out_dtype)
```

```python
import functools

import jax
import jax.numpy as jnp
from jax import lax
from jax.experimental import pallas as pl
from jax.experimental.pallas import tpu as pltpu

D_MODEL = 1024
POOL_WIDTH = 256
POOL_GROUP_DIM = 64
POOL_WINDOWS = (2, 4, 8, 16)
MAX_WINDOW = max(POOL_WINDOWS)
HEADS = 6
HEAD_DIM = 128
GDN_WIDTH = HEADS * HEAD_DIM
QKV_WIDTH = 3 * GDN_WIDTH
CONV_WIDTH = 4
D_FF = 2816
N_MOD = 6
EPS = 1e-6

LANES = 128
SUBLANES = 8
CHUNK = 128
BASE_BLOCK = 16
G_LANE = 8
SEQ_TILE = 256
FFN_TILE = 512
FFN_CHUNK = 256
VMEM_LIMIT = 56 * 1024 * 1024

BF16 = jnp.bfloat16
F32 = jnp.float32


def _sigmoid(x):
    return 1.0 / (1.0 + jnp.exp(-x))


def _silu(x):
    return x * _sigmoid(x)


def _softplus(x):
    return jnp.maximum(x, 0.0) + jnp.log(1.0 + jnp.exp(-jnp.abs(x)))


def _bdot(a, b):
    return jnp.dot(a.astype(BF16), b.astype(BF16), preferred_element_type=F32)


def _bdot_nt(a, b):
    return lax.dot_general(a.astype(BF16), b.astype(BF16),
                           (((1,), (1,)), ((), ())), preferred_element_type=F32)


def _bdot_tn(a, b):
    return lax.dot_general(a.astype(BF16), b.astype(BF16),
                           (((0,), (0,)), ((), ())), preferred_element_type=F32)


def _mod_kernel(c_ref, w_ref, b_ref, o_ref):
    cond = _silu(c_ref[...])
    o_ref[...] = jnp.dot(cond, w_ref[...], preferred_element_type=F32,
                         precision=lax.Precision.HIGHEST) + b_ref[...]


def _modulation(c, w_ada, b_ada):
    batch = c.shape[0]
    n = w_ada.shape[1]
    tn = D_MODEL
    return pl.pallas_call(
        _mod_kernel,
        out_shape=jax.ShapeDtypeStruct((batch, n), F32),
        grid=(n // tn,),
        in_specs=[
            pl.BlockSpec((batch, D_MODEL), lambda j: (0, 0)),
            pl.BlockSpec((D_MODEL, tn), lambda j: (0, j)),
            pl.BlockSpec((1, tn), lambda j: (0, j)),
        ],
        out_specs=pl.BlockSpec((batch, tn), lambda j: (0, j)),
        compiler_params=pltpu.CompilerParams(
            dimension_semantics=("arbitrary",), vmem_limit_bytes=VMEM_LIMIT),
        name="adaln_modulation",
    )(c, w_ada, b_ada.reshape(1, n))


def _chunk_cumsum(g):
    row = lax.broadcasted_iota(jnp.int32, g.shape, 0)
    shift = 1
    while shift < g.shape[0]:
        g = g + jnp.where(row >= shift, pltpu.roll(g, shift, 0), 0.0)
        shift *= 2
    return g


def _nested_block_masks(row, col):
    masks = []
    size = BASE_BLOCK
    inside = (row // size) == (col // size)
    masks.append(inside)
    while size < CHUNK:
        size *= 2
        outer = (row // size) == (col // size)
        masks.append(outer & ~inside)
        inside = outer
    return masks


def _unit_lower_inverse_minus_identity(lower, block_masks):
    neg = -jnp.where(block_masks[0], lower, 0.0)
    inv = neg
    power = neg.astype(BF16)
    span = 2
    while span < BASE_BLOCK:
        power_f = jnp.dot(power, power, preferred_element_type=F32)
        power = power_f.astype(BF16)
        inv = inv + power_f + jnp.dot(inv.astype(BF16), power, preferred_element_type=F32)
        span *= 2
    for mask in block_masks[1:]:
        off = jnp.where(mask, lower, 0.0)
        inv_b = inv.astype(BF16)
        half = off + jnp.dot(inv_b, off.astype(BF16), preferred_element_type=F32)
        inv = inv - half - jnp.dot(half.astype(BF16), inv_b, preferred_element_type=F32)
    return inv


def _mix_kernel(x_ref, mod_ref, gain_ref, w_u_ref, w_qkv_ref, w_z_ref, w_ba_ref,
                conv_w_ref, a_log_ref, dt_bias_ref, o_gain_ref, w_pool_ref,
                pool_scale_ref, w_out_ref, out_ref,
                qkv_buf, act_buf, z_buf, u_buf, y_buf, state_ref):
    ts = x_ref.shape[1]
    halo = SUBLANES
    tail = CONV_WIDTH - 1

    @pl.when(pl.program_id(1) == 0)
    def _():
        qkv_buf[0:halo, :] = jnp.zeros((halo, QKV_WIDTH), F32)
        u_buf[0:MAX_WINDOW, :] = jnp.zeros((MAX_WINDOW, POOL_WIDTH), F32)
        state_ref[...] = jnp.zeros_like(state_ref)

    x = x_ref[0]
    mod = mod_ref[0]
    shift_m, scale_m, gate_m = mod[0:1], mod[1:2], mod[2:3]
    inv_rms = lax.rsqrt(jnp.mean(x * x, axis=-1, keepdims=True) + EPS)
    hn = ((x * inv_rms) * (gain_ref[...] * (1.0 + scale_m)) + shift_m).astype(BF16)

    u_buf[MAX_WINDOW:MAX_WINDOW + ts, :] = jnp.dot(
        hn, w_u_ref[...], preferred_element_type=F32)
    qkv_buf[halo:halo + ts, :] = jnp.dot(hn, w_qkv_ref[...], preferred_element_type=F32)
    ba = jnp.dot(hn, w_ba_ref[...], preferred_element_type=F32)
    z_buf[...] = _silu(jnp.dot(hn, w_z_ref[...], preferred_element_type=F32))

    lane = lax.broadcasted_iota(jnp.int32, (ts, LANES), 1)
    upper = lane >= POOL_GROUP_DIM
    pos = (lax.broadcasted_iota(jnp.int32, (ts, LANES), 0)
           + pl.program_id(1) * ts + 1)
    pooled = []
    for half in range(POOL_WIDTH // LANES):
        w_lo, w_hi = POOL_WINDOWS[2 * half], POOL_WINDOWS[2 * half + 1]
        cols = slice(half * LANES, (half + 1) * LANES)
        cur = u_buf[MAX_WINDOW:MAX_WINDOW + ts, cols]
        acc_lo = cur
        for j in range(1, w_lo):
            acc_lo = acc_lo + u_buf[MAX_WINDOW - j:MAX_WINDOW - j + ts, cols]
        acc_hi = u_buf[MAX_WINDOW - w_lo:MAX_WINDOW - w_lo + ts, cols]
        for j in range(w_lo + 1, w_hi):
            acc_hi = acc_hi + u_buf[MAX_WINDOW - j:MAX_WINDOW - j + ts, cols]
        total = acc_lo + jnp.where(upper, acc_hi, 0.0)
        count = jnp.minimum(pos, jnp.where(upper, w_hi, w_lo)).astype(F32)
        pooled.append(total / count - cur)
    pooled = jnp.concatenate(pooled, axis=-1)
    u_buf[0:MAX_WINDOW, :] = u_buf[ts:ts + MAX_WINDOW, :]
    mixed = _bdot(pooled, w_pool_ref[...])
    mixed = mixed * lax.rsqrt(jnp.mean(mixed * mixed, axis=-1, keepdims=True) + EPS)
    y_buf[:, 0:POOL_WIDTH] = (mixed * pool_scale_ref[...]).astype(BF16)

    conv = conv_w_ref[CONV_WIDTH - 1:CONV_WIDTH, :] * qkv_buf[halo:halo + ts, :]
    for j in range(tail):
        lo = halo - tail + j
        conv = conv + conv_w_ref[j:j + 1, :] * qkv_buf[lo:lo + ts, :]
    qkv_buf[halo - tail:halo, :] = qkv_buf[halo + ts - tail:halo + ts, :]
    act_buf[...] = _silu(conv)

    beta_all = _sigmoid(ba)
    g_all = -jnp.exp(a_log_ref[...]) * _softplus(ba + dt_bias_ref[...])

    row = lax.broadcasted_iota(jnp.int32, (CHUNK, CHUNK), 0)
    col = lax.broadcasted_iota(jnp.int32, (CHUNK, CHUNK), 1)
    causal = row >= col
    strict = row > col
    block_masks = _nested_block_masks(row, col)

    for c in range(ts // CHUNK):
        rows = slice(c * CHUNK, (c + 1) * CHUNK)
        g_cum = _chunk_cumsum(g_all[rows])
        g_cum_t = g_cum.T
        beta_c = beta_all[rows]
        for h in range(HEADS):
            hcol = slice(h * HEAD_DIM, (h + 1) * HEAD_DIM)
            q = act_buf[rows, hcol]
            k = act_buf[rows, GDN_WIDTH + h * HEAD_DIM:GDN_WIDTH + (h + 1) * HEAD_DIM]
            v = act_buf[rows, 2 * GDN_WIDTH + h * HEAD_DIM:2 * GDN_WIDTH + (h + 1) * HEAD_DIM]
            q = q * (lax.rsqrt(jnp.sum(q * q, axis=-1, keepdims=True) + EPS)
                     * (HEAD_DIM ** -0.5))
            k = k * lax.rsqrt(jnp.sum(k * k, axis=-1, keepdims=True) + EPS)

            beta = jnp.broadcast_to(beta_c[:, h:h + 1], (CHUNK, HEAD_DIM))
            g_col = jnp.broadcast_to(g_cum[:, G_LANE + h:G_LANE + h + 1], (CHUNK, CHUNK))
            g_row = jnp.broadcast_to(g_cum_t[G_LANE + h:G_LANE + h + 1, :], (CHUNK, CHUNK))
            g_last = g_col[CHUNK - 1:CHUNK, :]
            decay = jnp.where(causal, jnp.exp(jnp.where(causal, g_col - g_row, 0.0)), 0.0)
            exp_g = jnp.exp(g_col)

            kb = k * beta
            scores = _bdot_nt(jnp.concatenate([kb, q], axis=0), k)
            lower = jnp.where(strict, scores[:CHUNK] * decay, 0.0)
            qk = scores[CHUNK:] * decay

            rhs = jnp.concatenate([v * beta, kb * exp_g], axis=-1)
            sol = rhs + _bdot(_unit_lower_inverse_minus_identity(lower, block_masks), rhs)
            u_c, w_c = sol[:, :HEAD_DIM], sol[:, HEAD_DIM:]

            state = state_ref[h]
            ws = _bdot(jnp.concatenate([w_c, q * exp_g], axis=0), state)
            v_new = u_c - ws[:CHUNK]
            o = ws[CHUNK:] + _bdot(qk, v_new)
            k_dec = k * jnp.exp(g_last - g_col)
            state_ref[h] = state * jnp.exp(g_last) + _bdot_tn(k_dec, v_new)

            o = o * lax.rsqrt(jnp.mean(o * o, axis=-1, keepdims=True) + EPS)
            y_buf[rows, POOL_WIDTH + h * HEAD_DIM:POOL_WIDTH + (h + 1) * HEAD_DIM] = (
                (o * o_gain_ref[...]) * z_buf[rows, hcol]).astype(BF16)

    mix = jnp.dot(y_buf[...], w_out_ref[...], preferred_element_type=F32)
    out_ref[0] = x_ref[0] + gate_m * mix


def _token_mixing(x, mod, gain, w_u, w_qkv, w_z, w_ba, conv_w, a_log, dt_bias, o_gain,
                  w_pool, pool_scale, w_out):
    batch, seq, _ = x.shape
    ts = SEQ_TILE
    const = lambda shape: pl.BlockSpec(shape, lambda b, s: (0,) * len(shape))
    return pl.pallas_call(
        _mix_kernel,
        out_shape=jax.ShapeDtypeStruct(x.shape, F32),
        grid=(batch, seq // ts),
        in_specs=[
            pl.BlockSpec((1, ts, D_MODEL), lambda b, s: (b, s, 0)),
            pl.BlockSpec((1, N_MOD, D_MODEL), lambda b, s: (b, 0, 0)),
            const((1, D_MODEL)),
            const((D_MODEL, POOL_WIDTH)),
            const((D_MODEL, QKV_WIDTH)),
            const((D_MODEL, GDN_WIDTH)),
            const((D_MODEL, LANES)),
            const((CONV_WIDTH, QKV_WIDTH)),
            const((1, LANES)),
            const((1, LANES)),
            const((1, HEAD_DIM)),
            const((POOL_WIDTH, POOL_WIDTH)),
            const((1, POOL_WIDTH)),
            const((D_MODEL, D_MODEL)),
        ],
        out_specs=pl.BlockSpec((1, ts, D_MODEL), lambda b, s: (b, s, 0)),
        scratch_shapes=[
            pltpu.VMEM((SUBLANES + ts, QKV_WIDTH), F32),
            pltpu.VMEM((ts, QKV_WIDTH), F32),
            pltpu.VMEM((ts, GDN_WIDTH), F32),
            pltpu.VMEM((MAX_WINDOW + ts, POOL_WIDTH), F32),
            pltpu.VMEM((ts, D_MODEL), BF16),
            pltpu.VMEM((HEADS, HEAD_DIM, HEAD_DIM), F32),
        ],
        compiler_params=pltpu.CompilerParams(
            dimension_semantics=("parallel", "arbitrary"), vmem_limit_bytes=VMEM_LIMIT),
        name="token_mixing",
    )(x, mod, gain, w_u, w_qkv, w_z, w_ba, conv_w, a_log, dt_bias, o_gain, w_pool,
      pool_scale, w_out)


def _ffn_kernel(h_ref, mod_ref, gain_ref, w_gate_ref, w_up_ref, w_down_ref,
                final_gain_ref, out_ref):
    h = h_ref[0]
    mod = mod_ref[0]
    shift_f, scale_f, gate_f = mod[3:4], mod[4:5], mod[5:6]
    inv_rms = lax.rsqrt(jnp.mean(h * h, axis=-1, keepdims=True) + EPS)
    hn = ((h * inv_rms) * (gain_ref[...] * (1.0 + scale_f)) + shift_f).astype(BF16)
    acc = jnp.zeros(h.shape, F32)
    for j in range(D_FF // FFN_CHUNK):
        cols = slice(j * FFN_CHUNK, (j + 1) * FFN_CHUNK)
        gate = jnp.dot(hn, w_gate_ref[:, cols], preferred_element_type=F32)
        up = jnp.dot(hn, w_up_ref[:, cols], preferred_element_type=F32)
        act = (_silu(gate) * up).astype(BF16)
        acc = acc + jnp.dot(act, w_down_ref[cols, :], preferred_element_type=F32)
    h2 = h + gate_f * acc
    inv_rms2 = lax.rsqrt(jnp.mean(h2 * h2, axis=-1, keepdims=True) + EPS)
    out_ref[0] = (h2 * inv_rms2) * final_gain_ref[...]


def _ffn(h, mod, gain, w_gate, w_up, w_down, final_gain):
    batch, seq, _ = h.shape
    tm = FFN_TILE
    const = lambda shape: pl.BlockSpec(shape, lambda b, s: (0,) * len(shape),
                                       pipeline_mode=pl.Buffered(1))
    return pl.pallas_call(
        _ffn_kernel,
        out_shape=jax.ShapeDtypeStruct(h.shape, F32),
        grid=(batch, seq // tm),
        in_specs=[
            pl.BlockSpec((1, tm, D_MODEL), lambda b, s: (b, s, 0)),
            pl.BlockSpec((1, N_MOD, D_MODEL), lambda b, s: (b, 0, 0)),
            const((1, D_MODEL)),
            const((D_MODEL, D_FF)),
            const((D_MODEL, D_FF)),
            const((D_FF, D_MODEL)),
            const((1, D_MODEL)),
        ],
        out_specs=pl.BlockSpec((1, tm, D_MODEL), lambda b, s: (b, s, 0)),
        compiler_params=pltpu.CompilerParams(
            dimension_semantics=("parallel", "parallel"), vmem_limit_bytes=VMEM_LIMIT),
        name="swiglu_final_norm",
    )(h, mod, gain, w_gate, w_up, w_down, final_gain)


def _lane_row(values, offset):
    return jnp.zeros((1, LANES), F32).at[0, offset:offset + values.shape[0]].set(
        values.astype(F32))


def kernel(x, c, w_ada, b_ada, norm_mix_gain, w_in, conv_w, a_log, dt_bias, gdn_norm_gain,
           w_pool, pool_scale, w_out, norm_ffn_gain, w_gate_up, w_down, norm_final_gain):
    depth = w_ada.shape[0]
    assert depth == 1, "the SwiGLU call applies the final norm, so only one layer is supported"
    batch = x.shape[0]
    h = x.astype(F32)
    for l in range(depth):
        mod = _modulation(c.astype(F32), w_ada[l], b_ada[l]).reshape(batch, N_MOD, D_MODEL)

        o_qkv = POOL_WIDTH
        o_z = o_qkv + QKV_WIDTH
        o_b = o_z + GDN_WIDTH
        o_a = o_b + HEADS
        w = w_in[l]
        w_ba = jnp.zeros((D_MODEL, LANES), F32)
        w_ba = w_ba.at[:, 0:HEADS].set(w[:, o_b:o_a])
        w_ba = w_ba.at[:, G_LANE:G_LANE + HEADS].set(w[:, o_a:o_a + HEADS])
        w_pool_bd = jnp.zeros((POOL_WIDTH, POOL_WIDTH), F32)
        for g in range(len(POOL_WINDOWS)):
            sl = slice(g * POOL_GROUP_DIM, (g + 1) * POOL_GROUP_DIM)
            w_pool_bd = w_pool_bd.at[sl, sl].set(w_pool[l, g])

        h = _token_mixing(
            h, mod, norm_mix_gain[l].reshape(1, D_MODEL),
            w[:, 0:o_qkv].astype(BF16), w[:, o_qkv:o_z].astype(BF16),
            w[:, o_z:o_b].astype(BF16), w_ba.astype(BF16),
            conv_w[l].astype(F32),
            _lane_row(a_log[l], G_LANE),
            _lane_row(dt_bias[l], G_LANE),
            gdn_norm_gain[l].reshape(1, HEAD_DIM).astype(F32),
            w_pool_bd.astype(BF16), pool_scale[l].reshape(1, POOL_WIDTH).astype(F32),
            w_out[l].astype(BF16))
        h = _ffn(h, mod, norm_ffn_gain[l].reshape(1, D_MODEL),
                 w_gate_up[l][:, :D_FF].astype(BF16), w_gate_up[l][:, D_FF:].astype(BF16),
                 w_down[l].astype(BF16), norm_final_gain.reshape(1, D_MODEL).astype(F32))
    return h.astype(x.dtype)
```

```python
import functools

import jax
import jax.numpy as jnp
from jax import lax
from jax.experimental import pallas as pl
from jax.experimental.pallas import tpu as pltpu

D_MODEL = 1024
POOL_WIDTH = 256
POOL_GROUP_DIM = 64
POOL_WINDOWS = (2, 4, 8, 16)
MAX_WINDOW = max(POOL_WINDOWS)
HEADS = 6
HEAD_DIM = 128
GDN_WIDTH = HEADS * HEAD_DIM
QKV_WIDTH = 3 * GDN_WIDTH
CONV_WIDTH = 4
D_FF = 2816
N_MOD = 6
EPS = 1e-6

LANES = 128
SUBLANES = 8
CHUNK = 128
BASE_BLOCK = 16
G_LANE = 8
SEQ_TILE = 256
FFN_TILE = 512
FFN_CHUNK = 256
VMEM_LIMIT = 56 * 1024 * 1024

BF16 = jnp.bfloat16
F32 = jnp.float32


def _sigmoid(x):
    return 1.0 / (1.0 + jnp.exp(-x))


def _silu(x):
    return x * _sigmoid(x)


def _softplus(x):
    return jnp.maximum(x, 0.0) + jnp.log(1.0 + jnp.exp(-jnp.abs(x)))


def _bdot(a, b):
    return jnp.dot(a.astype(BF16), b.astype(BF16), preferred_element_type=F32)


def _bdot_nt(a, b):
    return lax.dot_general(a.astype(BF16), b.astype(BF16),
                           (((1,), (1,)), ((), ())), preferred_element_type=F32)


def _bdot_tn(a, b):
    return lax.dot_general(a.astype(BF16), b.astype(BF16),
                           (((0,), (0,)), ((), ())), preferred_element_type=F32)


def _mod_kernel(c_ref, w_ref, b_ref, o_ref):
    cond = _silu(c_ref[...])
    o_ref[...] = jnp.dot(cond, w_ref[...], preferred_element_type=F32,
                         precision=lax.Precision.HIGHEST) + b_ref[...]


def _modulation(c, w_ada, b_ada):
    batch = c.shape[0]
    n = w_ada.shape[1]
    tn = D_MODEL
    return pl.pallas_call(
        _mod_kernel,
        out_shape=jax.ShapeDtypeStruct((batch, n), F32),
        grid=(n // tn,),
        in_specs=[
            pl.BlockSpec((batch, D_MODEL), lambda j: (0, 0)),
            pl.BlockSpec((D_MODEL, tn), lambda j: (0, j)),
            pl.BlockSpec((1, tn), lambda j: (0, j)),
        ],
        out_specs=pl.BlockSpec((batch, tn), lambda j: (0, j)),
        compiler_params=pltpu.CompilerParams(
            dimension_semantics=("arbitrary",), vmem_limit_bytes=VMEM_LIMIT),
        name="adaln_modulation",
    )(c, w_ada, b_ada.reshape(1, n))


def _chunk_cumsum(g):
    row = lax.broadcasted_iota(jnp.int32, g.shape, 0)
    shift = 1
    while shift < g.shape[0]:
        g = g + jnp.where(row >= shift, pltpu.roll(g, shift, 0), 0.0)
        shift *= 2
    return g


def _nested_block_masks(row, col):
    masks = []
    size = BASE_BLOCK
    inside = (row // size) == (col // size)
    masks.append(inside)
    while size < CHUNK:
        size *= 2
        outer = (row // size) == (col // size)
        masks.append(outer & ~inside)
        inside = outer
    return masks


def _unit_lower_inverses_minus_identity(lowers, block_masks):
    n = range(len(lowers))
    invs = [-jnp.where(block_masks[0], l, 0.0) for l in lowers]
    powers = [inv.astype(BF16) for inv in invs]
    span = 2
    while span < BASE_BLOCK:
        powers_f = [jnp.dot(p, p, preferred_element_type=F32) for p in powers]
        powers = [p.astype(BF16) for p in powers_f]
        prods = [jnp.dot(invs[i].astype(BF16), powers[i], preferred_element_type=F32)
                 for i in n]
        invs = [invs[i] + powers_f[i] + prods[i] for i in n]
        span *= 2
    for mask in block_masks[1:]:
        invs_b = [inv.astype(BF16) for inv in invs]
        offs = [jnp.where(mask, l, 0.0) for l in lowers]
        halves = [offs[i] + jnp.dot(invs_b[i], offs[i].astype(BF16),
                                    preferred_element_type=F32) for i in n]
        corrs = [jnp.dot(halves[i].astype(BF16), invs_b[i], preferred_element_type=F32)
                 for i in n]
        invs = [invs[i] - halves[i] - corrs[i] for i in n]
    return invs


def _mix_kernel(x_ref, mod_ref, gain_ref, w_u_ref, w_qkv_ref, w_z_ref, w_ba_ref,
                conv_w_ref, a_log_ref, dt_bias_ref, o_gain_ref, w_pool_ref,
                pool_scale_ref, w_out_ref, out_ref,
                qkv_buf, act_buf, z_buf, u_buf, y_buf, state_ref):
    ts = x_ref.shape[1]
    halo = SUBLANES
    tail = CONV_WIDTH - 1

    @pl.when(pl.program_id(1) == 0)
    def _():
        qkv_buf[0:halo, :] = jnp.zeros((halo, QKV_WIDTH), F32)
        u_buf[0:MAX_WINDOW, :] = jnp.zeros((MAX_WINDOW, POOL_WIDTH), F32)
        state_ref[...] = jnp.zeros_like(state_ref)

    x = x_ref[0]
    mod = mod_ref[0]
    shift_m, scale_m, gate_m = mod[0:1], mod[1:2], mod[2:3]
    inv_rms = lax.rsqrt(jnp.mean(x * x, axis=-1, keepdims=True) + EPS)
    hn = ((x * inv_rms) * (gain_ref[...] * (1.0 + scale_m)) + shift_m).astype(BF16)

    u_buf[MAX_WINDOW:MAX_WINDOW + ts, :] = jnp.dot(
        hn, w_u_ref[...], preferred_element_type=F32)
    qkv_buf[halo:halo + ts, :] = jnp.dot(hn, w_qkv_ref[...], preferred_element_type=F32)
    ba = jnp.dot(hn, w_ba_ref[...], preferred_element_type=F32)
    z_buf[...] = _silu(jnp.dot(hn, w_z_ref[...], preferred_element_type=F32))

    lane = lax.broadcasted_iota(jnp.int32, (ts, LANES), 1)
    upper = lane >= POOL_GROUP_DIM
    pos = (lax.broadcasted_iota(jnp.int32, (ts, LANES), 0)
           + pl.program_id(1) * ts + 1)
    pooled = []
    for half in range(POOL_WIDTH // LANES):
        w_lo, w_hi = POOL_WINDOWS[2 * half], POOL_WINDOWS[2 * half + 1]
        cols = slice(half * LANES, (half + 1) * LANES)
        cur = u_buf[MAX_WINDOW:MAX_WINDOW + ts, cols]
        acc_lo = cur
        for j in range(1, w_lo):
            acc_lo = acc_lo + u_buf[MAX_WINDOW - j:MAX_WINDOW - j + ts, cols]
        acc_hi = u_buf[MAX_WINDOW - w_lo:MAX_WINDOW - w_lo + ts, cols]
        for j in range(w_lo + 1, w_hi):
            acc_hi = acc_hi + u_buf[MAX_WINDOW - j:MAX_WINDOW - j + ts, cols]
        total = acc_lo + jnp.where(upper, acc_hi, 0.0)
        count = jnp.minimum(pos, jnp.where(upper, w_hi, w_lo)).astype(F32)
        pooled.append(total / count - cur)
    pooled = jnp.concatenate(pooled, axis=-1)
    u_buf[0:MAX_WINDOW, :] = u_buf[ts:ts + MAX_WINDOW, :]
    mixed = _bdot(pooled, w_pool_ref[...])
    mixed = mixed * lax.rsqrt(jnp.mean(mixed * mixed, axis=-1, keepdims=True) + EPS)
    y_buf[:, 0:POOL_WIDTH] = (mixed * pool_scale_ref[...]).astype(BF16)

    conv = conv_w_ref[CONV_WIDTH - 1:CONV_WIDTH, :] * qkv_buf[halo:halo + ts, :]
    for j in range(tail):
        lo = halo - tail + j
        conv = conv + conv_w_ref[j:j + 1, :] * qkv_buf[lo:lo + ts, :]
    qkv_buf[halo - tail:halo, :] = qkv_buf[halo + ts - tail:halo + ts, :]
    act_buf[...] = _silu(conv)

    beta_all = _sigmoid(ba)
    g_all = -jnp.exp(a_log_ref[...]) * _softplus(ba + dt_bias_ref[...])

    row = lax.broadcasted_iota(jnp.int32, (CHUNK, CHUNK), 0)
    col = lax.broadcasted_iota(jnp.int32, (CHUNK, CHUNK), 1)
    causal = row >= col
    strict = row > col
    block_masks = _nested_block_masks(row, col)

    n_chunks = ts // CHUNK
    lowers, qks, rhss, q_decs, k_decs, last_decays = [], [], [], [], [], []
    for c in range(n_chunks):
        rows = slice(c * CHUNK, (c + 1) * CHUNK)
        g_cum = _chunk_cumsum(g_all[rows])
        g_cum_t = g_cum.T
        beta_c = beta_all[rows]
        for h in range(HEADS):
            hcol = slice(h * HEAD_DIM, (h + 1) * HEAD_DIM)
            q = act_buf[rows, hcol]
            k = act_buf[rows, GDN_WIDTH + h * HEAD_DIM:GDN_WIDTH + (h + 1) * HEAD_DIM]
            v = act_buf[rows, 2 * GDN_WIDTH + h * HEAD_DIM:2 * GDN_WIDTH + (h + 1) * HEAD_DIM]
            q = q * (lax.rsqrt(jnp.sum(q * q, axis=-1, keepdims=True) + EPS)
                     * (HEAD_DIM ** -0.5))
            k = k * lax.rsqrt(jnp.sum(k * k, axis=-1, keepdims=True) + EPS)

            beta = jnp.broadcast_to(beta_c[:, h:h + 1], (CHUNK, HEAD_DIM))
            g_col = jnp.broadcast_to(g_cum[:, G_LANE + h:G_LANE + h + 1], (CHUNK, CHUNK))
            g_row = jnp.broadcast_to(g_cum_t[G_LANE + h:G_LANE + h + 1, :], (CHUNK, CHUNK))
            g_last = g_col[CHUNK - 1:CHUNK, :]
            decay = jnp.where(causal, jnp.exp(jnp.where(causal, g_col - g_row, 0.0)), 0.0)
            exp_g = jnp.exp(g_col)

            kb = k * beta
            scores = _bdot_nt(jnp.concatenate([kb, q], axis=0), k)
            lowers.append(jnp.where(strict, scores[:CHUNK] * decay, 0.0))
            qks.append((scores[CHUNK:] * decay).astype(BF16))
            rhss.append(jnp.concatenate([v * beta, kb * exp_g], axis=-1))
            q_decs.append((q * exp_g).astype(BF16))
            k_decs.append((k * jnp.exp(g_last - g_col)).astype(BF16))
            last_decays.append(jnp.exp(g_last))

    invs = _unit_lower_inverses_minus_identity(lowers, block_masks)
    sols = [rhss[i] + _bdot(invs[i], rhss[i]) for i in range(len(invs))]

    states = [state_ref[h] for h in range(HEADS)]
    for c in range(n_chunks):
        rows = slice(c * CHUNK, (c + 1) * CHUNK)
        idx = [c * HEADS + h for h in range(HEADS)]
        wss = [_bdot(jnp.concatenate([sols[i][:, HEAD_DIM:].astype(BF16), q_decs[i]], axis=0),
                     states[h]) for h, i in enumerate(idx)]
        v_news = [sols[i][:, :HEAD_DIM] - wss[h][:CHUNK] for h, i in enumerate(idx)]
        v_news_b = [v.astype(BF16) for v in v_news]
        outs = [wss[h][CHUNK:] + jnp.dot(qks[i], v_news_b[h], preferred_element_type=F32)
                for h, i in enumerate(idx)]
        states = [states[h] * last_decays[i] + _bdot_tn(k_decs[i], v_news_b[h])
                  for h, i in enumerate(idx)]
        for h in range(HEADS):
            hcol = slice(h * HEAD_DIM, (h + 1) * HEAD_DIM)
            o = outs[h]
            o = o * lax.rsqrt(jnp.mean(o * o, axis=-1, keepdims=True) + EPS)
            y_buf[rows, POOL_WIDTH + h * HEAD_DIM:POOL_WIDTH + (h + 1) * HEAD_DIM] = (
                (o * o_gain_ref[...]) * z_buf[rows, hcol]).astype(BF16)
    for h in range(HEADS):
        state_ref[h] = states[h]

    mix = jnp.dot(y_buf[...], w_out_ref[...], preferred_element_type=F32)
    out_ref[0] = x_ref[0] + gate_m * mix


def _token_mixing(x, mod, gain, w_u, w_qkv, w_z, w_ba, conv_w, a_log, dt_bias, o_gain,
                  w_pool, pool_scale, w_out):
    batch, seq, _ = x.shape
    ts = SEQ_TILE
    const = lambda shape: pl.BlockSpec(shape, lambda b, s: (0,) * len(shape))
    return pl.pallas_call(
        _mix_kernel,
        out_shape=jax.ShapeDtypeStruct(x.shape, F32),
        grid=(batch, seq // ts),
        in_specs=[
            pl.BlockSpec((1, ts, D_MODEL), lambda b, s: (b, s, 0)),
            pl.BlockSpec((1, N_MOD, D_MODEL), lambda b, s: (b, 0, 0)),
            const((1, D_MODEL)),
            const((D_MODEL, POOL_WIDTH)),
            const((D_MODEL, QKV_WIDTH)),
            const((D_MODEL, GDN_WIDTH)),
            const((D_MODEL, LANES)),
            const((CONV_WIDTH, QKV_WIDTH)),
            const((1, LANES)),
            const((1, LANES)),
            const((1, HEAD_DIM)),
            const((POOL_WIDTH, POOL_WIDTH)),
            const((1, POOL_WIDTH)),
            const((D_MODEL, D_MODEL)),
        ],
        out_specs=pl.BlockSpec((1, ts, D_MODEL), lambda b, s: (b, s, 0)),
        scratch_shapes=[
            pltpu.VMEM((SUBLANES + ts, QKV_WIDTH), F32),
            pltpu.VMEM((ts, QKV_WIDTH), F32),
            pltpu.VMEM((ts, GDN_WIDTH), F32),
            pltpu.VMEM((MAX_WINDOW + ts, POOL_WIDTH), F32),
            pltpu.VMEM((ts, D_MODEL), BF16),
            pltpu.VMEM((HEADS, HEAD_DIM, HEAD_DIM), F32),
        ],
        compiler_params=pltpu.CompilerParams(
            dimension_semantics=("parallel", "arbitrary"), vmem_limit_bytes=VMEM_LIMIT),
        name="token_mixing",
    )(x, mod, gain, w_u, w_qkv, w_z, w_ba, conv_w, a_log, dt_bias, o_gain, w_pool,
      pool_scale, w_out)


def _ffn_kernel(h_ref, mod_ref, gain_ref, w_gate_ref, w_up_ref, w_down_ref,
                final_gain_ref, out_ref):
    h = h_ref[0]
    mod = mod_ref[0]
    shift_f, scale_f, gate_f = mod[3:4], mod[4:5], mod[5:6]
    inv_rms = lax.rsqrt(jnp.mean(h * h, axis=-1, keepdims=True) + EPS)
    hn = ((h * inv_rms) * (gain_ref[...] * (1.0 + scale_f)) + shift_f).astype(BF16)
    acc = jnp.zeros(h.shape, F32)
    for j in range(D_FF // FFN_CHUNK):
        cols = slice(j * FFN_CHUNK, (j + 1) * FFN_CHUNK)
        gate = jnp.dot(hn, w_gate_ref[:, cols], preferred_element_type=F32)
        up = jnp.dot(hn, w_up_ref[:, cols], preferred_element_type=F32)
        act = (_silu(gate) * up).astype(BF16)
        acc = acc + jnp.dot(act, w_down_ref[cols, :], preferred_element_type=F32)
    h2 = h + gate_f * acc
    inv_rms2 = lax.rsqrt(jnp.mean(h2 * h2, axis=-1, keepdims=True) + EPS)
    out_ref[0] = (h2 * inv_rms2) * final_gain_ref[...]


def _ffn(h, mod, gain, w_gate, w_up, w_down, final_gain):
    batch, seq, _ = h.shape
    tm = FFN_TILE
    const = lambda shape: pl.BlockSpec(shape, lambda b, s: (0,) * len(shape),
                                       pipeline_mode=pl.Buffered(1))
    return pl.pallas_call(
        _ffn_kernel,
        out_shape=jax.ShapeDtypeStruct(h.shape, F32),
        grid=(batch, seq // tm),
        in_specs=[
            pl.BlockSpec((1, tm, D_MODEL), lambda b, s: (b, s, 0)),
            pl.BlockSpec((1, N_MOD, D_MODEL), lambda b, s: (b, 0, 0)),
            const((1, D_MODEL)),
            const((D_MODEL, D_FF)),
            const((D_MODEL, D_FF)),
            const((D_FF, D_MODEL)),
            const((1, D_MODEL)),
        ],
        out_specs=pl.BlockSpec((1, tm, D_MODEL), lambda b, s: (b, s, 0)),
        compiler_params=pltpu.CompilerParams(
            dimension_semantics=("parallel", "parallel"), vmem_limit_bytes=VMEM_LIMIT),
        name="swiglu_final_norm",
    )(h, mod, gain, w_gate, w_up, w_down, final_gain)


def _lane_row(values, offset):
    return jnp.zeros((1, LANES), F32).at[0, offset:offset + values.shape[0]].set(
        values.astype(F32))


def kernel(x, c, w_ada, b_ada, norm_mix_gain, w_in, conv_w, a_log, dt_bias, gdn_norm_gain,
           w_pool, pool_scale, w_out, norm_ffn_gain, w_gate_up, w_down, norm_final_gain):
    depth = w_ada.shape[0]
    assert depth == 1, "the SwiGLU call applies the final norm, so only one layer is supported"
    batch = x.shape[0]
    h = x.astype(F32)
    for l in range(depth):
        mod = _modulation(c.astype(F32), w_ada[l], b_ada[l]).reshape(batch, N_MOD, D_MODEL)

        o_qkv = POOL_WIDTH
        o_z = o_qkv + QKV_WIDTH
        o_b = o_z + GDN_WIDTH
        o_a = o_b + HEADS
        w = w_in[l]
        w_ba = jnp.zeros((D_MODEL, LANES), F32)
        w_ba = w_ba.at[:, 0:HEADS].set(w[:, o_b:o_a])
        w_ba = w_ba.at[:, G_LANE:G_LANE + HEADS].set(w[:, o_a:o_a + HEADS])
        w_pool_bd = jnp.zeros((POOL_WIDTH, POOL_WIDTH), F32)
        for g in range(len(POOL_WINDOWS)):
            sl = slice(g * POOL_GROUP_DIM, (g + 1) * POOL_GROUP_DIM)
            w_pool_bd = w_pool_bd.at[sl, sl].set(w_pool[l, g])

        h = _token_mixing(
            h, mod, norm_mix_gain[l].reshape(1, D_MODEL),
            w[:, 0:o_qkv].astype(BF16), w[:, o_qkv:o_z].astype(BF16),
            w[:, o_z:o_b].astype(BF16), w_ba.astype(BF16),
            conv_w[l].astype(F32),
            _lane_row(a_log[l], G_LANE),
            _lane_row(dt_bias[l], G_LANE),
            gdn_norm_gain[l].reshape(1, HEAD_DIM).astype(F32),
            w_pool_bd.astype(BF16), pool_scale[l].reshape(1, POOL_WIDTH).astype(F32),
            w_out[l].astype(BF16))
        h = _ffn(h, mod, norm_ffn_gain[l].reshape(1, D_MODEL),
                 w_gate_up[l][:, :D_FF].astype(BF16), w_gate_up[l][:, D_FF:].astype(BF16),
                 w_down[l].astype(BF16), norm_final_gain.reshape(1, D_MODEL).astype(F32))
    return h.astype(x.dtype)
```

```python
import functools

import jax
import jax.numpy as jnp
from jax import lax
from jax.experimental import pallas as pl
from jax.experimental.pallas import tpu as pltpu

D_MODEL = 1024
POOL_WIDTH = 256
POOL_GROUP_DIM = 64
POOL_WINDOWS = (2, 4, 8, 16)
MAX_WINDOW = max(POOL_WINDOWS)
HEADS = 6
HEAD_DIM = 128
GDN_WIDTH = HEADS * HEAD_DIM
QKV_WIDTH = 3 * GDN_WIDTH
CONV_WIDTH = 4
D_FF = 2816
N_MOD = 6
EPS = 1e-6

LANES = 128
SUBLANES = 8
CHUNK = 128
BASE_BLOCK = 16
G_LANE = 8
SEQ_TILE = 256
FFN_CHUNK = 256
VMEM_LIMIT = 56 * 1024 * 1024

BF16 = jnp.bfloat16
F32 = jnp.float32


def _sigmoid(x):
    return 1.0 / (1.0 + jnp.exp(-x))


def _silu(x):
    return x * _sigmoid(x)


def _softplus(x):
    return jnp.maximum(x, 0.0) + jnp.log(1.0 + jnp.exp(-jnp.abs(x)))


def _bdot(a, b):
    return jnp.dot(a.astype(BF16), b.astype(BF16), preferred_element_type=F32)


def _bdot_nt(a, b):
    return lax.dot_general(a.astype(BF16), b.astype(BF16),
                           (((1,), (1,)), ((), ())), preferred_element_type=F32)


def _bdot_tn(a, b):
    return lax.dot_general(a.astype(BF16), b.astype(BF16),
                           (((0,), (0,)), ((), ())), preferred_element_type=F32)


def _mod_kernel(c_ref, w_ref, b_ref, o_ref):
    cond = _silu(c_ref[...])
    o_ref[...] = jnp.dot(cond, w_ref[...], preferred_element_type=F32,
                         precision=lax.Precision.HIGHEST) + b_ref[...]


def _modulation(c, w_ada, b_ada):
    batch = c.shape[0]
    n = w_ada.shape[1]
    tn = D_MODEL
    return pl.pallas_call(
        _mod_kernel,
        out_shape=jax.ShapeDtypeStruct((batch, n), F32),
        grid=(n // tn,),
        in_specs=[
            pl.BlockSpec((batch, D_MODEL), lambda j: (0, 0)),
            pl.BlockSpec((D_MODEL, tn), lambda j: (0, j)),
            pl.BlockSpec((1, tn), lambda j: (0, j)),
        ],
        out_specs=pl.BlockSpec((batch, tn), lambda j: (0, j)),
        compiler_params=pltpu.CompilerParams(
            dimension_semantics=("arbitrary",), vmem_limit_bytes=VMEM_LIMIT),
        name="adaln_modulation",
    )(c, w_ada, b_ada.reshape(1, n))


def _chunk_cumsum(g):
    row = lax.broadcasted_iota(jnp.int32, g.shape, 0)
    shift = 1
    while shift < g.shape[0]:
        g = g + jnp.where(row >= shift, pltpu.roll(g, shift, 0), 0.0)
        shift *= 2
    return g


def _nested_block_masks(row, col):
    masks = []
    size = BASE_BLOCK
    inside = (row // size) == (col // size)
    masks.append(inside)
    while size < CHUNK:
        size *= 2
        outer = (row // size) == (col // size)
        masks.append(outer & ~inside)
        inside = outer
    return masks


def _unit_lower_inverses_minus_identity(lowers, block_masks, tick):
    n = range(len(lowers))
    invs = [-jnp.where(block_masks[0], l, 0.0) for l in lowers]
    powers = [inv.astype(BF16) for inv in invs]
    span = 2
    while span < BASE_BLOCK:
        powers_f = [jnp.dot(p, p, preferred_element_type=F32) for p in powers]
        powers = [p.astype(BF16) for p in powers_f]
        prods = [jnp.dot(invs[i].astype(BF16), powers[i], preferred_element_type=F32)
                 for i in n]
        invs = [invs[i] + powers_f[i] + prods[i] for i in n]
        span *= 2
        tick()
    for mask in block_masks[1:]:
        invs_b = [inv.astype(BF16) for inv in invs]
        offs = [jnp.where(mask, l, 0.0) for l in lowers]
        halves = [offs[i] + jnp.dot(invs_b[i], offs[i].astype(BF16),
                                    preferred_element_type=F32) for i in n]
        corrs = [jnp.dot(halves[i].astype(BF16), invs_b[i], preferred_element_type=F32)
                 for i in n]
        invs = [invs[i] - halves[i] - corrs[i] for i in n]
        tick()
    return invs


def _swiglu_stages(h_buf, act_buf, mod_ref, gain_ref, w_gate_ref, w_up_ref, w_down_ref,
                   final_gain_ref, out_ref):
    h = h_buf[...]
    mod = mod_ref[0]
    shift_f, scale_f, gate_f = mod[3:4], mod[4:5], mod[5:6]
    inv_rms = lax.rsqrt(jnp.mean(h * h, axis=-1, keepdims=True) + EPS)
    hn = ((h * inv_rms) * (gain_ref[...] * (1.0 + scale_f)) + shift_f).astype(BF16)
    yield
    for j in range(D_FF // FFN_CHUNK):
        cols = slice(j * FFN_CHUNK, (j + 1) * FFN_CHUNK)
        gate = jnp.dot(hn, w_gate_ref[:, cols], preferred_element_type=F32)
        up = jnp.dot(hn, w_up_ref[:, cols], preferred_element_type=F32)
        act_buf[:, cols] = (_silu(gate) * up).astype(BF16)
        yield
    acc = []
    for j in range(D_MODEL // FFN_CHUNK):
        cols = slice(j * FFN_CHUNK, (j + 1) * FFN_CHUNK)
        acc.append(jnp.dot(act_buf[...], w_down_ref[:, cols], preferred_element_type=F32))
        yield
    h2 = h + gate_f * jnp.concatenate(acc, axis=-1)
    inv_rms2 = lax.rsqrt(jnp.mean(h2 * h2, axis=-1, keepdims=True) + EPS)
    out_ref[0] = (h2 * inv_rms2) * final_gain_ref[...]


def _block_kernel(tiles_per_seq,
                  x_ref, mod_mix_ref, mod_ffn_ref, gain_ref, w_u_ref, w_qkv_ref, w_z_ref,
                  w_ba_ref, conv_w_ref, a_log_ref, dt_bias_ref, o_gain_ref, w_pool_ref,
                  pool_scale_ref, w_out_ref, ffn_gain_ref, w_gate_ref, w_up_ref,
                  w_down_ref, final_gain_ref, out_ref,
                  qkv_buf, act_buf, z_buf, u_buf, y_buf, state_ref, h_buf, ffn_act_buf):
    ts = x_ref.shape[1]
    halo = SUBLANES
    tail = CONV_WIDTH - 1
    step = pl.program_id(0)
    seq_step = step % tiles_per_seq

    @pl.when(seq_step == 0)
    def _():
        qkv_buf[0:halo, :] = jnp.zeros((halo, QKV_WIDTH), F32)
        u_buf[0:MAX_WINDOW, :] = jnp.zeros((MAX_WINDOW, POOL_WIDTH), F32)
        state_ref[...] = jnp.zeros_like(state_ref)

    @pl.when(step == 0)
    def _():
        h_buf[...] = jnp.zeros_like(h_buf)

    swiglu = _swiglu_stages(h_buf, ffn_act_buf, mod_ffn_ref, ffn_gain_ref, w_gate_ref,
                            w_up_ref, w_down_ref, final_gain_ref, out_ref)

    def tick(n=1):
        for _ in range(n):
            next(swiglu, None)

    tick()
    x = x_ref[0]
    mod = mod_mix_ref[0]
    shift_m, scale_m, gate_m = mod[0:1], mod[1:2], mod[2:3]
    inv_rms = lax.rsqrt(jnp.mean(x * x, axis=-1, keepdims=True) + EPS)
    hn = ((x * inv_rms) * (gain_ref[...] * (1.0 + scale_m)) + shift_m).astype(BF16)

    u_buf[MAX_WINDOW:MAX_WINDOW + ts, :] = jnp.dot(
        hn, w_u_ref[...], preferred_element_type=F32)
    qkv_buf[halo:halo + ts, :] = jnp.dot(hn, w_qkv_ref[...], preferred_element_type=F32)
    ba = jnp.dot(hn, w_ba_ref[...], preferred_element_type=F32)
    z_buf[...] = _silu(jnp.dot(hn, w_z_ref[...], preferred_element_type=F32))

    lane = lax.broadcasted_iota(jnp.int32, (ts, LANES), 1)
    upper = lane >= POOL_GROUP_DIM
    pos = lax.broadcasted_iota(jnp.int32, (ts, LANES), 0) + seq_step * ts + 1
    pooled = []
    for half in range(POOL_WIDTH // LANES):
        w_lo, w_hi = POOL_WINDOWS[2 * half], POOL_WINDOWS[2 * half + 1]
        cols = slice(half * LANES, (half + 1) * LANES)
        cur = u_buf[MAX_WINDOW:MAX_WINDOW + ts, cols]
        acc_lo = cur
        for j in range(1, w_lo):
            acc_lo = acc_lo + u_buf[MAX_WINDOW - j:MAX_WINDOW - j + ts, cols]
        acc_hi = u_buf[MAX_WINDOW - w_lo:MAX_WINDOW - w_lo + ts, cols]
        for j in range(w_lo + 1, w_hi):
            acc_hi = acc_hi + u_buf[MAX_WINDOW - j:MAX_WINDOW - j + ts, cols]
        total = acc_lo + jnp.where(upper, acc_hi, 0.0)
        count = jnp.minimum(pos, jnp.where(upper, w_hi, w_lo)).astype(F32)
        pooled.append(total / count - cur)
    pooled = jnp.concatenate(pooled, axis=-1)
    u_buf[0:MAX_WINDOW, :] = u_buf[ts:ts + MAX_WINDOW, :]
    mixed = _bdot(pooled, w_pool_ref[...])
    mixed = mixed * lax.rsqrt(jnp.mean(mixed * mixed, axis=-1, keepdims=True) + EPS)
    y_buf[:, 0:POOL_WIDTH] = (mixed * pool_scale_ref[...]).astype(BF16)
    tick(2)

    for blk in range(QKV_WIDTH // GDN_WIDTH):
        cols = slice(blk * GDN_WIDTH, (blk + 1) * GDN_WIDTH)
        conv = conv_w_ref[CONV_WIDTH - 1:CONV_WIDTH, cols] * qkv_buf[halo:halo + ts, cols]
        for j in range(tail):
            lo = halo - tail + j
            conv = conv + conv_w_ref[j:j + 1, cols] * qkv_buf[lo:lo + ts, cols]
        act_buf[:, cols] = _silu(conv)
        tick(3)
    qkv_buf[halo - tail:halo, :] = qkv_buf[halo + ts - tail:halo + ts, :]

    beta_all = _sigmoid(ba)
    g_all = -jnp.exp(a_log_ref[...]) * _softplus(ba + dt_bias_ref[...])

    row = lax.broadcasted_iota(jnp.int32, (CHUNK, CHUNK), 0)
    col = lax.broadcasted_iota(jnp.int32, (CHUNK, CHUNK), 1)
    causal = row >= col
    strict = row > col
    block_masks = _nested_block_masks(row, col)

    n_chunks = ts // CHUNK
    lowers, qks, rhss, q_decs, k_decs, last_decays = [], [], [], [], [], []
    for c in range(n_chunks):
        rows = slice(c * CHUNK, (c + 1) * CHUNK)
        g_cum = _chunk_cumsum(g_all[rows])
        g_cum_t = g_cum.T
        beta_c = beta_all[rows]
        for h in range(HEADS):
            hcol = slice(h * HEAD_DIM, (h + 1) * HEAD_DIM)
            q = act_buf[rows, hcol]
            k = act_buf[rows, GDN_WIDTH + h * HEAD_DIM:GDN_WIDTH + (h + 1) * HEAD_DIM]
            v = act_buf[rows, 2 * GDN_WIDTH + h * HEAD_DIM:2 * GDN_WIDTH + (h + 1) * HEAD_DIM]
            q = q * (lax.rsqrt(jnp.sum(q * q, axis=-1, keepdims=True) + EPS)
                     * (HEAD_DIM ** -0.5))
            k = k * lax.rsqrt(jnp.sum(k * k, axis=-1, keepdims=True) + EPS)

            beta = jnp.broadcast_to(beta_c[:, h:h + 1], (CHUNK, HEAD_DIM))
            g_col = jnp.broadcast_to(g_cum[:, G_LANE + h:G_LANE + h + 1], (CHUNK, CHUNK))
            g_row = jnp.broadcast_to(g_cum_t[G_LANE + h:G_LANE + h + 1, :], (CHUNK, CHUNK))
            g_last = g_col[CHUNK - 1:CHUNK, :]
            decay = jnp.where(causal, jnp.exp(jnp.where(causal, g_col - g_row, 0.0)), 0.0)
            exp_g = jnp.exp(g_col)

            kb = k * beta
            scores = _bdot_nt(jnp.concatenate([kb, q], axis=0), k)
            lowers.append(jnp.where(strict, scores[:CHUNK] * decay, 0.0))
            qks.append((scores[CHUNK:] * decay).astype(BF16))
            rhss.append(jnp.concatenate([v * beta, kb * exp_g], axis=-1))
            q_decs.append((q * exp_g).astype(BF16))
            k_decs.append((k * jnp.exp(g_last - g_col)).astype(BF16))
            last_decays.append(jnp.exp(g_last))
            if h % 3 == 2:
                tick()

    invs = _unit_lower_inverses_minus_identity(lowers, block_masks, tick)
    sols = [rhss[i] + _bdot(invs[i], rhss[i]) for i in range(len(invs))]

    states = [state_ref[h] for h in range(HEADS)]
    for c in range(n_chunks):
        rows = slice(c * CHUNK, (c + 1) * CHUNK)
        idx = [c * HEADS + h for h in range(HEADS)]
        wss = [_bdot(jnp.concatenate([sols[i][:, HEAD_DIM:].astype(BF16), q_decs[i]], axis=0),
                     states[h]) for h, i in enumerate(idx)]
        v_news = [sols[i][:, :HEAD_DIM] - wss[h][:CHUNK] for h, i in enumerate(idx)]
        v_news_b = [v.astype(BF16) for v in v_news]
        outs = [wss[h][CHUNK:] + jnp.dot(qks[i], v_news_b[h], preferred_element_type=F32)
                for h, i in enumerate(idx)]
        states = [states[h] * last_decays[i] + _bdot_tn(k_decs[i], v_news_b[h])
                  for h, i in enumerate(idx)]
        for h in range(HEADS):
            hcol = slice(h * HEAD_DIM, (h + 1) * HEAD_DIM)
            o = outs[h]
            o = o * lax.rsqrt(jnp.mean(o * o, axis=-1, keepdims=True) + EPS)
            y_buf[rows, POOL_WIDTH + h * HEAD_DIM:POOL_WIDTH + (h + 1) * HEAD_DIM] = (
                (o * o_gain_ref[...]) * z_buf[rows, hcol]).astype(BF16)
        tick()
    for h in range(HEADS):
        state_ref[h] = states[h]

    tick(D_FF // FFN_CHUNK + D_MODEL // FFN_CHUNK + 2)

    mix = jnp.dot(y_buf[...], w_out_ref[...], preferred_element_type=F32)
    h_buf[...] = x_ref[0] + gate_m * mix


def _decoder_block(x, mod, gain, w_u, w_qkv, w_z, w_ba, conv_w, a_log, dt_bias, o_gain,
                   w_pool, pool_scale, w_out, ffn_gain, w_gate, w_up, w_down, final_gain):
    batch, seq, _ = x.shape
    ts = SEQ_TILE
    tiles_per_seq = seq // ts
    n_tiles = batch * tiles_per_seq

    def mix_tile(t):
        t = jnp.minimum(t, n_tiles - 1)
        return t // tiles_per_seq, t % tiles_per_seq

    def ffn_tile(t):
        t = jnp.maximum(t - 1, 0)
        return t // tiles_per_seq, t % tiles_per_seq

    def const(shape):
        return pl.BlockSpec(shape, lambda t: (0,) * len(shape), pipeline_mode=pl.Buffered(1))

    return pl.pallas_call(
        functools.partial(_block_kernel, tiles_per_seq),
        out_shape=jax.ShapeDtypeStruct(x.shape, F32),
        grid=(n_tiles + 1,),
        in_specs=[
            pl.BlockSpec((1, ts, D_MODEL), lambda t: (*mix_tile(t), 0)),
            pl.BlockSpec((1, N_MOD, D_MODEL), lambda t: (mix_tile(t)[0], 0, 0)),
            pl.BlockSpec((1, N_MOD, D_MODEL), lambda t: (ffn_tile(t)[0], 0, 0)),
            const((1, D_MODEL)),
            const((D_MODEL, POOL_WIDTH)),
            const((D_MODEL, QKV_WIDTH)),
            const((D_MODEL, GDN_WIDTH)),
            const((D_MODEL, LANES)),
            const((CONV_WIDTH, QKV_WIDTH)),
            const((1, LANES)),
            const((1, LANES)),
            const((1, HEAD_DIM)),
            const((POOL_WIDTH, POOL_WIDTH)),
            const((1, POOL_WIDTH)),
            const((D_MODEL, D_MODEL)),
            const((1, D_MODEL)),
            const((D_MODEL, D_FF)),
            const((D_MODEL, D_FF)),
            const((D_FF, D_MODEL)),
            const((1, D_MODEL)),
        ],
        out_specs=pl.BlockSpec((1, ts, D_MODEL), lambda t: (*ffn_tile(t), 0)),
        scratch_shapes=[
            pltpu.VMEM((SUBLANES + ts, QKV_WIDTH), F32),
            pltpu.VMEM((ts, QKV_WIDTH), F32),
            pltpu.VMEM((ts, GDN_WIDTH), F32),
            pltpu.VMEM((MAX_WINDOW + ts, POOL_WIDTH), F32),
            pltpu.VMEM((ts, D_MODEL), BF16),
            pltpu.VMEM((HEADS, HEAD_DIM, HEAD_DIM), F32),
            pltpu.VMEM((ts, D_MODEL), F32),
            pltpu.VMEM((ts, D_FF), BF16),
        ],
        compiler_params=pltpu.CompilerParams(
            dimension_semantics=("arbitrary",), vmem_limit_bytes=VMEM_LIMIT),
        name="decoder_block",
    )(x, mod, mod, gain, w_u, w_qkv, w_z, w_ba, conv_w, a_log, dt_bias, o_gain, w_pool,
      pool_scale, w_out, ffn_gain, w_gate, w_up, w_down, final_gain)


def _lane_row(values, offset):
    return jnp.zeros((1, LANES), F32).at[0, offset:offset + values.shape[0]].set(
        values.astype(F32))


def kernel(x, c, w_ada, b_ada, norm_mix_gain, w_in, conv_w, a_log, dt_bias, gdn_norm_gain,
           w_pool, pool_scale, w_out, norm_ffn_gain, w_gate_up, w_down, norm_final_gain):
    depth = w_ada.shape[0]
    assert depth == 1, "the block call applies the final norm, so only one layer is supported"
    batch = x.shape[0]
    l = 0
    mod = _modulation(c.astype(F32), w_ada[l], b_ada[l]).reshape(batch, N_MOD, D_MODEL)

    o_qkv = POOL_WIDTH
    o_z = o_qkv + QKV_WIDTH
    o_b = o_z + GDN_WIDTH
    o_a = o_b + HEADS
    w = w_in[l]
    w_ba = jnp.zeros((D_MODEL, LANES), F32)
    w_ba = w_ba.at[:, 0:HEADS].set(w[:, o_b:o_a])
    w_ba = w_ba.at[:, G_LANE:G_LANE + HEADS].set(w[:, o_a:o_a + HEADS])
    w_pool_bd = jnp.zeros((POOL_WIDTH, POOL_WIDTH), F32)
    for g in range(len(POOL_WINDOWS)):
        sl = slice(g * POOL_GROUP_DIM, (g + 1) * POOL_GROUP_DIM)
        w_pool_bd = w_pool_bd.at[sl, sl].set(w_pool[l, g])

    out = _decoder_block(
        x.astype(F32), mod, norm_mix_gain[l].reshape(1, D_MODEL).astype(F32),
        w[:, 0:o_qkv].astype(BF16), w[:, o_qkv:o_z].astype(BF16),
        w[:, o_z:o_b].astype(BF16), w_ba.astype(BF16),
        conv_w[l].astype(F32),
        _lane_row(a_log[l], G_LANE),
        _lane_row(dt_bias[l], G_LANE),
        gdn_norm_gain[l].reshape(1, HEAD_DIM).astype(F32),
        w_pool_bd.astype(BF16), pool_scale[l].reshape(1, POOL_WIDTH).astype(F32),
        w_out[l].astype(BF16),
        norm_ffn_gain[l].reshape(1, D_MODEL).astype(F32),
        w_gate_up[l][:, :D_FF].astype(BF16), w_gate_up[l][:, D_FF:].astype(BF16),
        w_down[l].astype(BF16), norm_final_gain.reshape(1, D_MODEL).astype(F32))
    return out.astype(x.dtype)
```

```python
import functools

import jax
import jax.numpy as jnp
from jax import lax
from jax.experimental import pallas as pl
from jax.experimental.pallas import tpu as pltpu

D_MODEL = 1024
POOL_WIDTH = 256
POOL_GROUP_DIM = 64
POOL_WINDOWS = (2, 4, 8, 16)
HEADS = 6
HEAD_DIM = 128
GDN_WIDTH = HEADS * HEAD_DIM
QKV_WIDTH = 3 * GDN_WIDTH
CONV_WIDTH = 4
D_FF = 2816
N_MOD = 6
EPS = 1e-6

LANES = 128
SUBLANES = 8
CHUNK = 128
SEGMENT = CHUNK // SUBLANES
COL_BLOCKS = D_MODEL // LANES
BASE_BLOCK = 16
G_LANE = 8
SEQ_TILE = 256
FFN_CHUNK = 256
VMEM_LIMIT = 56 * 1024 * 1024
CONV_TAIL_GROUPS = CONV_WIDTH - 1
POOL_LEVELS = len(POOL_WINDOWS)
POOL_TAIL_GROUPS = POOL_WINDOWS[-1] // 2

BF16 = jnp.bfloat16
F32 = jnp.float32


def _sigmoid(x):
    return 1.0 / (1.0 + jnp.exp(-x))


def _silu(x):
    return x * _sigmoid(x)


def _softplus(x):
    return jnp.maximum(x, 0.0) + jnp.log(1.0 + jnp.exp(-jnp.abs(x)))


def _bdot(a, b):
    return jnp.dot(a.astype(BF16), b.astype(BF16), preferred_element_type=F32)


def _bdot_nt(a, b):
    return lax.dot_general(a.astype(BF16), b.astype(BF16),
                           (((1,), (1,)), ((), ())), preferred_element_type=F32)


def _bdot_tn(a, b):
    return lax.dot_general(a.astype(BF16), b.astype(BF16),
                           (((0,), (0,)), ((), ())), preferred_element_type=F32)


def _mod_kernel(c_ref, w_ref, b_ref, o_ref):
    cond = _silu(c_ref[...])
    o_ref[...] = jnp.dot(cond, w_ref[...], preferred_element_type=F32,
                         precision=lax.Precision.HIGHEST) + b_ref[...]


def _modulation(c, w_ada, b_ada):
    batch = c.shape[0]
    n = w_ada.shape[1]
    tn = D_MODEL
    return pl.pallas_call(
        _mod_kernel,
        out_shape=jax.ShapeDtypeStruct((batch, n), F32),
        grid=(n // tn,),
        in_specs=[
            pl.BlockSpec((batch, D_MODEL), lambda j: (0, 0)),
            pl.BlockSpec((D_MODEL, tn), lambda j: (0, j)),
            pl.BlockSpec((1, tn), lambda j: (0, j)),
        ],
        out_specs=pl.BlockSpec((batch, tn), lambda j: (0, j)),
        compiler_params=pltpu.CompilerParams(
            dimension_semantics=("arbitrary",), vmem_limit_bytes=VMEM_LIMIT),
        name="adaln_modulation",
    )(c, w_ada, b_ada.reshape(1, n))


def _row_group(a, k):
    return a[k * SUBLANES:(k + 1) * SUBLANES]


def _token_of_row(row):
    return (row % SUBLANES) * SEGMENT + row // SUBLANES


def _wrapped_tail(cur, prev_tail, groups):
    first = lax.broadcasted_iota(jnp.int32, (SUBLANES, cur.shape[1]), 0) == 0
    out = []
    for k in range(groups):
        mine = _row_group(cur, SEGMENT - groups + k)
        theirs = _row_group(prev_tail, k)
        out.append(jnp.where(first, pltpu.roll(theirs, 1, 0), pltpu.roll(mine, 1, 0)))
    return jnp.concatenate(out, axis=0)


def _shift_tokens(cur, wrapped, j):
    n = j * SUBLANES
    return jnp.concatenate([wrapped[wrapped.shape[0] - n:], cur[:CHUNK - n]], axis=0)


def _chunk_cumsum(g):
    run = _row_group(g, 0)
    prefixes = [run]
    for k in range(1, SEGMENT):
        run = run + _row_group(g, k)
        prefixes.append(run)
    sub = lax.broadcasted_iota(jnp.int32, run.shape, 0)
    offset = jnp.where(sub >= 1, pltpu.roll(run, 1, 0), 0.0)
    shift = 1
    while shift < SUBLANES:
        offset = offset + jnp.where(sub >= shift, pltpu.roll(offset, shift, 0), 0.0)
        shift *= 2
    return jnp.concatenate([p + offset for p in prefixes], axis=0)


def _nested_block_masks(row, col):
    masks = []
    size = BASE_BLOCK
    inside = (row // size) == (col // size)
    masks.append(inside)
    while size < CHUNK:
        size *= 2
        outer = (row // size) == (col // size)
        masks.append(outer & ~inside)
        inside = outer
    return masks


def _unit_lower_inverses_minus_identity(lowers, block_masks, tick):
    n = range(len(lowers))
    invs = [-jnp.where(block_masks[0], l, 0.0) for l in lowers]
    powers = [inv.astype(BF16) for inv in invs]
    span = 2
    while span < BASE_BLOCK:
        powers_f = [jnp.dot(p, p, preferred_element_type=F32) for p in powers]
        powers = [p.astype(BF16) for p in powers_f]
        prods = [jnp.dot(invs[i].astype(BF16), powers[i], preferred_element_type=F32)
                 for i in n]
        invs = [invs[i] + powers_f[i] + prods[i] for i in n]
        span *= 2
        tick()
    for mask in block_masks[1:]:
        invs_b = [inv.astype(BF16) for inv in invs]
        offs = [jnp.where(mask, l, 0.0) for l in lowers]
        halves = [offs[i] + jnp.dot(invs_b[i], offs[i].astype(BF16),
                                    preferred_element_type=F32) for i in n]
        corrs = [jnp.dot(halves[i].astype(BF16), invs_b[i], preferred_element_type=F32)
                 for i in n]
        invs = [invs[i] - halves[i] - corrs[i] for i in n]
        tick()
    return invs


def _swiglu_stages(h_buf, act_buf, mod_ref, gain_ref, w_gate_ref, w_up_ref, w_down_ref,
                   final_gain_ref, out_ref):
    h = jnp.concatenate([h_buf[blk] for blk in range(COL_BLOCKS)], axis=-1)
    mod = mod_ref[0]
    shift_f, scale_f, gate_f = mod[3:4], mod[4:5], mod[5:6]
    inv_rms = lax.rsqrt(jnp.mean(h * h, axis=-1, keepdims=True) + EPS)
    hn = ((h * inv_rms) * (gain_ref[...] * (1.0 + scale_f)) + shift_f).astype(BF16)
    yield
    for j in range(D_FF // FFN_CHUNK):
        cols = slice(j * FFN_CHUNK, (j + 1) * FFN_CHUNK)
        gate = jnp.dot(hn, w_gate_ref[:, cols], preferred_element_type=F32)
        up = jnp.dot(hn, w_up_ref[:, cols], preferred_element_type=F32)
        act_buf[:, cols] = (_silu(gate) * up).astype(BF16)
        yield
    acc = []
    for j in range(D_MODEL // FFN_CHUNK):
        acc.append(jnp.dot(act_buf[...], w_down_ref[:, j * FFN_CHUNK:(j + 1) * FFN_CHUNK],
                           preferred_element_type=F32))
        yield
    h2 = h + gate_f * jnp.concatenate(acc, axis=-1)
    inv_rms2 = lax.rsqrt(jnp.mean(h2 * h2, axis=-1, keepdims=True) + EPS)
    out_ref[0] = (h2 * inv_rms2) * final_gain_ref[...]


def _load_segment_major(x_refs, n_chunks):
    cols = []
    for x_ref in x_refs:
        groups = []
        for c in range(n_chunks):
            for r in range(SEGMENT):
                groups.append(x_ref[0, pl.ds(c * CHUNK + r, SUBLANES, stride=SEGMENT), :])
        cols.append(jnp.concatenate(groups, axis=0))
    return jnp.concatenate(cols, axis=-1)


def _block_kernel(tiles_per_seq, *refs):
    x_refs = refs[:COL_BLOCKS]
    (mod_mix_ref, mod_ffn_ref, gain_ref, w_u_ref, w_qkv_ref, w_z_ref,
     w_ba_ref, conv_w_ref, a_log_ref, dt_bias_ref, o_gain_ref, w_pool_ref,
     pool_scale_ref, w_out_ref, ffn_gain_ref, w_gate_ref, w_up_ref,
     w_down_ref, final_gain_ref, out_ref,
     qkv_buf, act_buf, z_buf, conv_tail, pool_tail, y_buf, state_ref, h_buf,
     ffn_act_buf, x_prev) = refs[COL_BLOCKS:]
    ts = out_ref.shape[1]
    n_chunks = ts // CHUNK
    step = pl.program_id(0)
    seq_step = step % tiles_per_seq

    @pl.when(seq_step == 0)
    def _():
        conv_tail[...] = jnp.zeros_like(conv_tail)
        pool_tail[...] = jnp.zeros_like(pool_tail)
        state_ref[...] = jnp.zeros_like(state_ref)

    @pl.when(step == 0)
    def _():
        y_buf[...] = jnp.zeros_like(y_buf)
        x_prev[...] = jnp.zeros_like(x_prev)

    gate_prev = mod_ffn_ref[0][2:3]
    mix = jnp.dot(y_buf[...], w_out_ref[...], preferred_element_type=F32)
    h_prev = x_prev[...] + gate_prev * mix
    for blk in range(COL_BLOCKS):
        for c in range(n_chunks):
            for r in range(SEGMENT):
                lo = c * CHUNK + r * SUBLANES
                h_buf[blk, pl.ds(c * CHUNK + r, SUBLANES, stride=SEGMENT), :] = (
                    h_prev[lo:lo + SUBLANES, blk * LANES:(blk + 1) * LANES])

    swiglu = _swiglu_stages(h_buf, ffn_act_buf, mod_ffn_ref, ffn_gain_ref, w_gate_ref,
                            w_up_ref, w_down_ref, final_gain_ref, out_ref)

    def tick(n=1):
        for _ in range(n):
            next(swiglu, None)

    x = _load_segment_major(x_refs, n_chunks)
    x_prev[...] = x
    mod = mod_mix_ref[0]
    shift_m, scale_m = mod[0:1], mod[1:2]
    inv_rms = lax.rsqrt(jnp.mean(x * x, axis=-1, keepdims=True) + EPS)
    hn = ((x * inv_rms) * (gain_ref[...] * (1.0 + scale_m)) + shift_m).astype(BF16)
    tick()

    u = jnp.dot(hn, w_u_ref[...], preferred_element_type=F32)
    qkv_buf[...] = jnp.dot(hn, w_qkv_ref[...], preferred_element_type=F32)
    ba = jnp.dot(hn, w_ba_ref[...], preferred_element_type=F32)
    z_buf[...] = _silu(jnp.dot(hn, w_z_ref[...], preferred_element_type=F32))

    lane = lax.broadcasted_iota(jnp.int32, (CHUNK, POOL_WIDTH), 1)
    window = jnp.full((CHUNK, POOL_WIDTH), POOL_WINDOWS[0], jnp.int32)
    for gi in range(1, POOL_LEVELS):
        window = jnp.where(lane >= gi * POOL_GROUP_DIM, POOL_WINDOWS[gi], window)
    chunk_token = _token_of_row(lax.broadcasted_iota(jnp.int32, (CHUNK, POOL_WIDTH), 0))
    pooled = []
    tail_lo = [(POOL_TAIL_GROUPS - w // 2) * SUBLANES for w in POOL_WINDOWS]
    prev_tails = [pool_tail[level, tail_lo[level]:, :] for level in range(POOL_LEVELS)]
    for c in range(n_chunks):
        u_c = u[c * CHUNK:(c + 1) * CHUNK]
        sums = u_c
        total = None
        for level in range(POOL_LEVELS):
            j = POOL_WINDOWS[level] // 2
            tail = prev_tails[level]
            prev_tails[level] = sums[CHUNK - j * SUBLANES:]
            sums = sums + _shift_tokens(sums, _wrapped_tail(sums, tail, j), j)
            total = sums if total is None else jnp.where(
                lane >= level * POOL_GROUP_DIM, sums, total)
        pos = chunk_token + (seq_step * ts + c * CHUNK + 1)
        count = jnp.minimum(pos, window).astype(F32)
        pooled.append(total / count - u_c)
    for level in range(POOL_LEVELS):
        pool_tail[level, tail_lo[level]:, :] = prev_tails[level]
    pooled = jnp.concatenate(pooled, axis=0)
    mixed = _bdot(pooled, w_pool_ref[...])
    mixed = mixed * lax.rsqrt(jnp.mean(mixed * mixed, axis=-1, keepdims=True) + EPS)
    y_buf[:, 0:POOL_WIDTH] = (mixed * pool_scale_ref[...]).astype(BF16)
    tick(2)

    tail_rows = CONV_TAIL_GROUPS * SUBLANES
    for blk in range(QKV_WIDTH // GDN_WIDTH):
        cols = slice(blk * GDN_WIDTH, (blk + 1) * GDN_WIDTH)
        prev_tail = conv_tail[:, cols]
        for c in range(n_chunks):
            rows = slice(c * CHUNK, (c + 1) * CHUNK)
            cur = qkv_buf[rows, cols]
            wrapped = _wrapped_tail(cur, prev_tail, CONV_TAIL_GROUPS)
            conv = conv_w_ref[CONV_WIDTH - 1:CONV_WIDTH, cols] * cur
            for j in range(1, CONV_WIDTH):
                conv = conv + (conv_w_ref[CONV_WIDTH - 1 - j:CONV_WIDTH - j, cols]
                               * _shift_tokens(cur, wrapped, j))
            act_buf[rows, cols] = _silu(conv)
            prev_tail = cur[CHUNK - tail_rows:]
        conv_tail[:, cols] = prev_tail
        tick(3)

    beta_all = _sigmoid(ba)
    g_all = -jnp.exp(a_log_ref[...]) * _softplus(ba + dt_bias_ref[...])

    row = _token_of_row(lax.broadcasted_iota(jnp.int32, (CHUNK, CHUNK), 0))
    col = _token_of_row(lax.broadcasted_iota(jnp.int32, (CHUNK, CHUNK), 1))
    causal = row >= col
    strict = row > col
    block_masks = _nested_block_masks(row, col)

    lowers, qks, rhss, q_decs, k_decs, last_decays = [], [], [], [], [], []
    for c in range(n_chunks):
        rows = slice(c * CHUNK, (c + 1) * CHUNK)
        g_cum = _chunk_cumsum(g_all[rows])
        g_cum_t = g_cum.T
        beta_c = beta_all[rows]
        for h in range(HEADS):
            hcol = slice(h * HEAD_DIM, (h + 1) * HEAD_DIM)
            q = act_buf[rows, hcol]
            k = act_buf[rows, GDN_WIDTH + h * HEAD_DIM:GDN_WIDTH + (h + 1) * HEAD_DIM]
            v = act_buf[rows, 2 * GDN_WIDTH + h * HEAD_DIM:2 * GDN_WIDTH + (h + 1) * HEAD_DIM]
            q = q * (lax.rsqrt(jnp.sum(q * q, axis=-1, keepdims=True) + EPS)
                     * (HEAD_DIM ** -0.5))
            k = k * lax.rsqrt(jnp.sum(k * k, axis=-1, keepdims=True) + EPS)

            beta = jnp.broadcast_to(beta_c[:, h:h + 1], (CHUNK, HEAD_DIM))
            g_col = jnp.broadcast_to(g_cum[:, G_LANE + h:G_LANE + h + 1], (CHUNK, CHUNK))
            g_row = jnp.broadcast_to(g_cum_t[G_LANE + h:G_LANE + h + 1, :], (CHUNK, CHUNK))
            g_last = g_col[CHUNK - 1:CHUNK, :]
            decay = jnp.where(causal, jnp.exp(jnp.where(causal, g_col - g_row, 0.0)), 0.0)
            exp_g = jnp.exp(g_col)

            kb = k * beta
            scores = _bdot_nt(jnp.concatenate([kb, q], axis=0), k)
            lowers.append(jnp.where(strict, scores[:CHUNK] * decay, 0.0))
            qks.append((scores[CHUNK:] * decay).astype(BF16))
            rhss.append(jnp.concatenate([v * beta, kb * exp_g], axis=-1))
            q_decs.append((q * exp_g).astype(BF16))
            k_decs.append((k * jnp.exp(g_last - g_col)).astype(BF16))
            last_decays.append(jnp.exp(g_last))
            if h % 3 == 2:
                tick()

    invs = _unit_lower_inverses_minus_identity(lowers, block_masks, tick)
    sols = [rhss[i] + _bdot(invs[i], rhss[i]) for i in range(len(invs))]

    states = [state_ref[h] for h in range(HEADS)]
    for c in range(n_chunks):
        rows = slice(c * CHUNK, (c + 1) * CHUNK)
        idx = [c * HEADS + h for h in range(HEADS)]
        wss = [_bdot(jnp.concatenate([sols[i][:, HEAD_DIM:].astype(BF16), q_decs[i]], axis=0),
                     states[h]) for h, i in enumerate(idx)]
        v_news = [sols[i][:, :HEAD_DIM] - wss[h][:CHUNK] for h, i in enumerate(idx)]
        v_news_b = [v.astype(BF16) for v in v_news]
        outs = [wss[h][CHUNK:] + jnp.dot(qks[i], v_news_b[h], preferred_element_type=F32)
                for h, i in enumerate(idx)]
        states = [states[h] * last_decays[i] + _bdot_tn(k_decs[i], v_news_b[h])
                  for h, i in enumerate(idx)]
        for h in range(HEADS):
            hcol = slice(h * HEAD_DIM, (h + 1) * HEAD_DIM)
            o = outs[h]
            o = o * lax.rsqrt(jnp.mean(o * o, axis=-1, keepdims=True) + EPS)
            y_buf[rows, POOL_WIDTH + h * HEAD_DIM:POOL_WIDTH + (h + 1) * HEAD_DIM] = (
                (o * o_gain_ref[...]) * z_buf[rows, hcol]).astype(BF16)
        tick()
    for h in range(HEADS):
        state_ref[h] = states[h]

    tick(D_FF // FFN_CHUNK + D_MODEL // FFN_CHUNK + 2)


def _decoder_block(x, mod, gain, w_u, w_qkv, w_z, w_ba, conv_w, a_log, dt_bias, o_gain,
                   w_pool, pool_scale, w_out, ffn_gain, w_gate, w_up, w_down, final_gain):
    batch, seq, _ = x.shape
    ts = SEQ_TILE
    tiles_per_seq = seq // ts
    n_tiles = batch * tiles_per_seq

    def tile(t, lag):
        t = jnp.clip(t - lag, 0, n_tiles - 1)
        return t // tiles_per_seq, t % tiles_per_seq

    mix_tile = functools.partial(tile, lag=0)
    ffn_tile = functools.partial(tile, lag=1)

    def x_block(blk, t):
        return (*mix_tile(t), blk)

    def const(shape):
        return pl.BlockSpec(shape, lambda t: (0,) * len(shape), pipeline_mode=pl.Buffered(1))

    return pl.pallas_call(
        functools.partial(_block_kernel, tiles_per_seq),
        out_shape=jax.ShapeDtypeStruct(x.shape, F32),
        grid=(n_tiles + 1,),
        in_specs=[
            *[pl.BlockSpec((1, ts, LANES), functools.partial(x_block, blk))
              for blk in range(COL_BLOCKS)],
            pl.BlockSpec((1, N_MOD, D_MODEL), lambda t: (mix_tile(t)[0], 0, 0)),
            pl.BlockSpec((1, N_MOD, D_MODEL), lambda t: (ffn_tile(t)[0], 0, 0)),
            const((1, D_MODEL)),
            const((D_MODEL, POOL_WIDTH)),
            const((D_MODEL, QKV_WIDTH)),
            const((D_MODEL, GDN_WIDTH)),
            const((D_MODEL, LANES)),
            const((CONV_WIDTH, QKV_WIDTH)),
            const((1, LANES)),
            const((1, LANES)),
            const((1, HEAD_DIM)),
            const((POOL_WIDTH, POOL_WIDTH)),
            const((1, POOL_WIDTH)),
            const((D_MODEL, D_MODEL)),
            const((1, D_MODEL)),
            const((D_MODEL, D_FF)),
            const((D_MODEL, D_FF)),
            const((D_FF, D_MODEL)),
            const((1, D_MODEL)),
        ],
        out_specs=pl.BlockSpec((1, ts, D_MODEL), lambda t: (*ffn_tile(t), 0)),
        scratch_shapes=[
            pltpu.VMEM((ts, QKV_WIDTH), F32),
            pltpu.VMEM((ts, QKV_WIDTH), F32),
            pltpu.VMEM((ts, GDN_WIDTH), F32),
            pltpu.VMEM((CONV_TAIL_GROUPS * SUBLANES, QKV_WIDTH), F32),
            pltpu.VMEM((POOL_LEVELS, POOL_TAIL_GROUPS * SUBLANES, POOL_WIDTH), F32),
            pltpu.VMEM((ts, D_MODEL), BF16),
            pltpu.VMEM((HEADS, HEAD_DIM, HEAD_DIM), F32),
            pltpu.VMEM((COL_BLOCKS, ts, LANES), F32),
            pltpu.VMEM((ts, D_FF), BF16),
            pltpu.VMEM((ts, D_MODEL), F32),
        ],
        compiler_params=pltpu.CompilerParams(
            dimension_semantics=("arbitrary",), vmem_limit_bytes=VMEM_LIMIT),
        name="decoder_block",
    )(*([x] * COL_BLOCKS), mod, mod, gain, w_u, w_qkv, w_z, w_ba, conv_w, a_log, dt_bias, o_gain, w_pool,
      pool_scale, w_out, ffn_gain, w_gate, w_up, w_down, final_gain)


def _lane_row(values, offset):
    return jnp.zeros((1, LANES), F32).at[0, offset:offset + values.shape[0]].set(
        values.astype(F32))


def kernel(x, c, w_ada, b_ada, norm_mix_gain, w_in, conv_w, a_log, dt_bias, gdn_norm_gain,
           w_pool, pool_scale, w_out, norm_ffn_gain, w_gate_up, w_down, norm_final_gain):
    depth = w_ada.shape[0]
    assert depth == 1, "the block call applies the final norm, so only one layer is supported"
    batch = x.shape[0]
    l = 0
    mod = _modulation(c.astype(F32), w_ada[l], b_ada[l]).reshape(batch, N_MOD, D_MODEL)

    o_qkv = POOL_WIDTH
    o_z = o_qkv + QKV_WIDTH
    o_b = o_z + GDN_WIDTH
    o_a = o_b + HEADS
    w = w_in[l]
    w_ba = jnp.zeros((D_MODEL, LANES), F32)
    w_ba = w_ba.at[:, 0:HEADS].set(w[:, o_b:o_a])
    w_ba = w_ba.at[:, G_LANE:G_LANE + HEADS].set(w[:, o_a:o_a + HEADS])
    w_pool_bd = jnp.zeros((POOL_WIDTH, POOL_WIDTH), F32)
    for g in range(len(POOL_WINDOWS)):
        sl = slice(g * POOL_GROUP_DIM, (g + 1) * POOL_GROUP_DIM)
        w_pool_bd = w_pool_bd.at[sl, sl].set(w_pool[l, g])

    out = _decoder_block(
        x.astype(F32), mod, norm_mix_gain[l].reshape(1, D_MODEL).astype(F32),
        w[:, 0:o_qkv].astype(BF16), w[:, o_qkv:o_z].astype(BF16),
        w[:, o_z:o_b].astype(BF16), w_ba.astype(BF16),
        conv_w[l].astype(F32),
        _lane_row(a_log[l], G_LANE),
        _lane_row(dt_bias[l], G_LANE),
        gdn_norm_gain[l].reshape(1, HEAD_DIM).astype(F32),
        w_pool_bd.astype(BF16), pool_scale[l].reshape(1, POOL_WIDTH).astype(F32),
        w_out[l].astype(BF16),
        norm_ffn_gain[l].reshape(1, D_MODEL).astype(F32),
        w_gate_up[l][:, :D_FF].astype(BF16), w_gate_up[l][:, D_FF:].astype(BF16),
        w_down[l].astype(BF16), norm_final_gain.reshape(1, D_MODEL).astype(F32))
    return out.astype(x.dtype)
```

```python
import functools

import jax
import jax.numpy as jnp
from jax import lax
from jax.experimental import pallas as pl
from jax.experimental.pallas import tpu as pltpu

D_MODEL = 1024
POOL_WIDTH = 256
POOL_GROUP_DIM = 64
POOL_WINDOWS = (2, 4, 8, 16)
HEADS = 6
HEAD_DIM = 128
GDN_WIDTH = HEADS * HEAD_DIM
QKV_WIDTH = 3 * GDN_WIDTH
CONV_WIDTH = 4
D_FF = 2816
N_MOD = 6
EPS = 1e-6

LANES = 128
SUBLANES = 8
CHUNK = 128
SEGMENT = CHUNK // SUBLANES
COL_BLOCKS = D_MODEL // LANES
BASE_BLOCK = 16
G_LANE = HEADS
O_QKV = POOL_WIDTH
O_Z = O_QKV + QKV_WIDTH
O_BA = O_Z + GDN_WIDTH
IN_COLS = O_BA + 2 * HEADS
IN_COLS_PADDED = O_BA + LANES
SEQ_TILE = 256
FFN_CHUNK = 256
VMEM_LIMIT = 56 * 1024 * 1024
CONV_TAIL_GROUPS = CONV_WIDTH - 1
POOL_LEVELS = len(POOL_WINDOWS)
POOL_TAIL_GROUPS = POOL_WINDOWS[-1] // 2

BF16 = jnp.bfloat16
F32 = jnp.float32


def _sigmoid(x):
    return 1.0 / (1.0 + jnp.exp(-x))


def _silu(x):
    return x * _sigmoid(x)


def _softplus(x):
    return jnp.maximum(x, 0.0) + jnp.log(1.0 + jnp.exp(-jnp.abs(x)))


def _bdot(a, b):
    return jnp.dot(a.astype(BF16), b.astype(BF16), preferred_element_type=F32)


def _bdot_nt(a, b):
    return lax.dot_general(a.astype(BF16), b.astype(BF16),
                           (((1,), (1,)), ((), ())), preferred_element_type=F32)


def _bdot_tn(a, b):
    return lax.dot_general(a.astype(BF16), b.astype(BF16),
                           (((0,), (0,)), ((), ())), preferred_element_type=F32)


def _mod_kernel(c_ref, w_ref, b_ref, o_ref):
    cond = _silu(c_ref[...])
    o_ref[...] = jnp.dot(cond, w_ref[...], preferred_element_type=F32,
                         precision=lax.Precision.HIGHEST) + b_ref[...]


def _modulation(c, w_ada, b_ada):
    batch = c.shape[0]
    n = w_ada.shape[1]
    tn = D_MODEL
    return pl.pallas_call(
        _mod_kernel,
        out_shape=jax.ShapeDtypeStruct((batch, n), F32),
        grid=(n // tn,),
        in_specs=[
            pl.BlockSpec((batch, D_MODEL), lambda j: (0, 0)),
            pl.BlockSpec((D_MODEL, tn), lambda j: (0, j)),
            pl.BlockSpec((1, tn), lambda j: (0, j)),
        ],
        out_specs=pl.BlockSpec((batch, tn), lambda j: (0, j)),
        compiler_params=pltpu.CompilerParams(
            dimension_semantics=("arbitrary",), vmem_limit_bytes=VMEM_LIMIT),
        name="adaln_modulation",
    )(c, w_ada, b_ada.reshape(1, n))


def _row_group(a, k):
    return a[k * SUBLANES:(k + 1) * SUBLANES]


def _token_of_row(row):
    return (row % SUBLANES) * SEGMENT + row // SUBLANES


def _wrapped_tail(cur, prev_tail, groups):
    first = lax.broadcasted_iota(jnp.int32, (SUBLANES, cur.shape[1]), 0) == 0
    out = []
    for k in range(groups):
        mine = _row_group(cur, SEGMENT - groups + k)
        theirs = _row_group(prev_tail, k)
        out.append(jnp.where(first, pltpu.roll(theirs, 1, 0), pltpu.roll(mine, 1, 0)))
    return jnp.concatenate(out, axis=0)


def _shift_tokens(cur, wrapped, j):
    n = j * SUBLANES
    return jnp.concatenate([wrapped[wrapped.shape[0] - n:], cur[:CHUNK - n]], axis=0)


def _chunk_cumsum(g):
    run = _row_group(g, 0)
    prefixes = [run]
    for k in range(1, SEGMENT):
        run = run + _row_group(g, k)
        prefixes.append(run)
    sub = lax.broadcasted_iota(jnp.int32, run.shape, 0)
    offset = jnp.where(sub >= 1, pltpu.roll(run, 1, 0), 0.0)
    shift = 1
    while shift < SUBLANES:
        offset = offset + jnp.where(sub >= shift, pltpu.roll(offset, shift, 0), 0.0)
        shift *= 2
    return jnp.concatenate([p + offset for p in prefixes], axis=0)


def _nested_block_masks(row, col):
    masks = []
    size = BASE_BLOCK
    inside = (row // size) == (col // size)
    masks.append(inside)
    while size < CHUNK:
        size *= 2
        outer = (row // size) == (col // size)
        masks.append(outer & ~inside)
        inside = outer
    return masks


def _unit_lower_inverses_minus_identity(lowers, block_masks, tick):
    n = range(len(lowers))
    invs = [-jnp.where(block_masks[0], l, 0.0) for l in lowers]
    powers = [inv.astype(BF16) for inv in invs]
    span = 2
    while span < BASE_BLOCK:
        powers_f = [jnp.dot(p, p, preferred_element_type=F32) for p in powers]
        powers = [p.astype(BF16) for p in powers_f]
        prods = [jnp.dot(invs[i].astype(BF16), powers[i], preferred_element_type=F32)
                 for i in n]
        invs = [invs[i] + powers_f[i] + prods[i] for i in n]
        span *= 2
        tick()
    for mask in block_masks[1:]:
        invs_b = [inv.astype(BF16) for inv in invs]
        offs = [jnp.where(mask, l, 0.0) for l in lowers]
        halves = [offs[i] + jnp.dot(invs_b[i], offs[i].astype(BF16),
                                    preferred_element_type=F32) for i in n]
        corrs = [jnp.dot(halves[i].astype(BF16), invs_b[i], preferred_element_type=F32)
                 for i in n]
        invs = [invs[i] - halves[i] - corrs[i] for i in n]
        tick()
    return invs


def _swiglu_stages(h_buf, act_buf, mod_ref, gain_ref, w_gate_up_ref, w_down_ref,
                   final_gain_ref, out_ref):
    h = jnp.concatenate([h_buf[blk] for blk in range(COL_BLOCKS)], axis=-1)
    mod = mod_ref[0]
    shift_f, scale_f, gate_f = mod[3:4], mod[4:5], mod[5:6]
    inv_rms = lax.rsqrt(jnp.mean(h * h, axis=-1, keepdims=True) + EPS)
    hn = ((h * inv_rms) * (gain_ref[...] * (1.0 + scale_f)) + shift_f).astype(BF16)
    yield
    for j in range(D_FF // FFN_CHUNK):
        cols = slice(j * FFN_CHUNK, (j + 1) * FFN_CHUNK)
        up_cols = slice(D_FF + j * FFN_CHUNK, D_FF + (j + 1) * FFN_CHUNK)
        gate = jnp.dot(hn, w_gate_up_ref[:, cols], preferred_element_type=F32)
        up = jnp.dot(hn, w_gate_up_ref[:, up_cols], preferred_element_type=F32)
        act_buf[:, cols] = (_silu(gate) * up).astype(BF16)
        yield
    acc = []
    for j in range(D_MODEL // FFN_CHUNK):
        acc.append(jnp.dot(act_buf[...], w_down_ref[:, j * FFN_CHUNK:(j + 1) * FFN_CHUNK],
                           preferred_element_type=F32))
        yield
    h2 = h + gate_f * jnp.concatenate(acc, axis=-1)
    inv_rms2 = lax.rsqrt(jnp.mean(h2 * h2, axis=-1, keepdims=True) + EPS)
    out_ref[0] = (h2 * inv_rms2) * final_gain_ref[...]


def _load_segment_major(x_refs, n_chunks):
    cols = []
    for x_ref in x_refs:
        groups = []
        for c in range(n_chunks):
            for r in range(SEGMENT):
                groups.append(x_ref[0, pl.ds(c * CHUNK + r, SUBLANES, stride=SEGMENT), :])
        cols.append(jnp.concatenate(groups, axis=0))
    return jnp.concatenate(cols, axis=-1)


def _block_kernel(tiles_per_seq, *refs):
    x_refs = refs[:COL_BLOCKS]
    (mod_mix_ref, mod_ffn_ref, gain_ref, w_in_ref,
     conv_w_ref, a_log_ref, dt_bias_ref, o_gain_ref, w_pool_ref,
     pool_scale_ref, w_out_ref, ffn_gain_ref, w_gate_up_ref,
     w_down_ref, final_gain_ref, out_ref,
     qkv_buf, act_buf, z_buf, conv_tail, pool_tail, y_buf, state_ref, h_buf,
     ffn_act_buf, x_prev) = refs[COL_BLOCKS:]
    ts = out_ref.shape[1]
    n_chunks = ts // CHUNK
    step = pl.program_id(0)
    seq_step = step % tiles_per_seq

    @pl.when(seq_step == 0)
    def _():
        conv_tail[...] = jnp.zeros_like(conv_tail)
        pool_tail[...] = jnp.zeros_like(pool_tail)
        state_ref[...] = jnp.zeros_like(state_ref)

    @pl.when(step == 0)
    def _():
        y_buf[...] = jnp.zeros_like(y_buf)
        x_prev[...] = jnp.zeros_like(x_prev)

    gate_prev = mod_ffn_ref[0][2:3]
    mix = jnp.dot(y_buf[...], w_out_ref[...], preferred_element_type=F32)
    h_prev = x_prev[...] + gate_prev * mix
    for blk in range(COL_BLOCKS):
        for c in range(n_chunks):
            for r in range(SEGMENT):
                lo = c * CHUNK + r * SUBLANES
                h_buf[blk, pl.ds(c * CHUNK + r, SUBLANES, stride=SEGMENT), :] = (
                    h_prev[lo:lo + SUBLANES, blk * LANES:(blk + 1) * LANES])

    swiglu = _swiglu_stages(h_buf, ffn_act_buf, mod_ffn_ref, ffn_gain_ref, w_gate_up_ref,
                            w_down_ref, final_gain_ref, out_ref)

    def tick(n=1):
        for _ in range(n):
            next(swiglu, None)

    x = _load_segment_major(x_refs, n_chunks)
    x_prev[...] = x
    mod = mod_mix_ref[0]
    shift_m, scale_m = mod[0:1], mod[1:2]
    inv_rms = lax.rsqrt(jnp.mean(x * x, axis=-1, keepdims=True) + EPS)
    hn = ((x * inv_rms) * (gain_ref[...] * (1.0 + scale_m)) + shift_m).astype(BF16)
    tick()

    u = jnp.dot(hn, w_in_ref[:, 0:O_QKV], preferred_element_type=F32)
    qkv_buf[...] = jnp.dot(hn, w_in_ref[:, O_QKV:O_Z], preferred_element_type=F32)
    ba = jnp.dot(hn, w_in_ref[:, O_BA:IN_COLS_PADDED], preferred_element_type=F32)
    z_buf[...] = _silu(jnp.dot(hn, w_in_ref[:, O_Z:O_BA], preferred_element_type=F32))

    lane = lax.broadcasted_iota(jnp.int32, (CHUNK, POOL_WIDTH), 1)
    window = jnp.full((CHUNK, POOL_WIDTH), POOL_WINDOWS[0], jnp.int32)
    for gi in range(1, POOL_LEVELS):
        window = jnp.where(lane >= gi * POOL_GROUP_DIM, POOL_WINDOWS[gi], window)
    chunk_token = _token_of_row(lax.broadcasted_iota(jnp.int32, (CHUNK, POOL_WIDTH), 0))
    pooled = []
    tail_lo = [(POOL_TAIL_GROUPS - w // 2) * SUBLANES for w in POOL_WINDOWS]
    prev_tails = [pool_tail[level, tail_lo[level]:, :] for level in range(POOL_LEVELS)]
    for c in range(n_chunks):
        u_c = u[c * CHUNK:(c + 1) * CHUNK]
        sums = u_c
        total = None
        for level in range(POOL_LEVELS):
            j = POOL_WINDOWS[level] // 2
            tail = prev_tails[level]
            prev_tails[level] = sums[CHUNK - j * SUBLANES:]
            sums = sums + _shift_tokens(sums, _wrapped_tail(sums, tail, j), j)
            total = sums if total is None else jnp.where(
                lane >= level * POOL_GROUP_DIM, sums, total)
        pos = chunk_token + (seq_step * ts + c * CHUNK + 1)
        count = jnp.minimum(pos, window).astype(F32)
        pooled.append(total / count - u_c)
    for level in range(POOL_LEVELS):
        pool_tail[level, tail_lo[level]:, :] = prev_tails[level]
    pooled = jnp.concatenate(pooled, axis=0)
    mixed = _bdot(pooled, w_pool_ref[...])
    mixed = mixed * lax.rsqrt(jnp.mean(mixed * mixed, axis=-1, keepdims=True) + EPS)
    y_buf[:, 0:POOL_WIDTH] = (mixed * pool_scale_ref[...]).astype(BF16)
    tick(2)

    tail_rows = CONV_TAIL_GROUPS * SUBLANES
    for blk in range(QKV_WIDTH // GDN_WIDTH):
        cols = slice(blk * GDN_WIDTH, (blk + 1) * GDN_WIDTH)
        prev_tail = conv_tail[:, cols]
        for c in range(n_chunks):
            rows = slice(c * CHUNK, (c + 1) * CHUNK)
            cur = qkv_buf[rows, cols]
            wrapped = _wrapped_tail(cur, prev_tail, CONV_TAIL_GROUPS)
            conv = conv_w_ref[CONV_WIDTH - 1:CONV_WIDTH, cols] * cur
            for j in range(1, CONV_WIDTH):
                conv = conv + (conv_w_ref[CONV_WIDTH - 1 - j:CONV_WIDTH - j, cols]
                               * _shift_tokens(cur, wrapped, j))
            act_buf[rows, cols] = _silu(conv)
            prev_tail = cur[CHUNK - tail_rows:]
        conv_tail[:, cols] = prev_tail
        tick(3)

    beta_all = _sigmoid(ba)
    g_all = -jnp.exp(a_log_ref[...]) * _softplus(ba + dt_bias_ref[...])

    row = _token_of_row(lax.broadcasted_iota(jnp.int32, (CHUNK, CHUNK), 0))
    col = _token_of_row(lax.broadcasted_iota(jnp.int32, (CHUNK, CHUNK), 1))
    causal = row >= col
    strict = row > col
    block_masks = _nested_block_masks(row, col)

    lowers, qks, rhss, q_decs, k_decs, last_decays = [], [], [], [], [], []
    for c in range(n_chunks):
        rows = slice(c * CHUNK, (c + 1) * CHUNK)
        g_cum = _chunk_cumsum(g_all[rows])
        g_cum_t = g_cum.T
        beta_c = beta_all[rows]
        for h in range(HEADS):
            hcol = slice(h * HEAD_DIM, (h + 1) * HEAD_DIM)
            q = act_buf[rows, hcol]
            k = act_buf[rows, GDN_WIDTH + h * HEAD_DIM:GDN_WIDTH + (h + 1) * HEAD_DIM]
            v = act_buf[rows, 2 * GDN_WIDTH + h * HEAD_DIM:2 * GDN_WIDTH + (h + 1) * HEAD_DIM]
            q = q * (lax.rsqrt(jnp.sum(q * q, axis=-1, keepdims=True) + EPS)
                     * (HEAD_DIM ** -0.5))
            k = k * lax.rsqrt(jnp.sum(k * k, axis=-1, keepdims=True) + EPS)

            beta = jnp.broadcast_to(beta_c[:, h:h + 1], (CHUNK, HEAD_DIM))
            g_col = jnp.broadcast_to(g_cum[:, G_LANE + h:G_LANE + h + 1], (CHUNK, CHUNK))
            g_row = jnp.broadcast_to(g_cum_t[G_LANE + h:G_LANE + h + 1, :], (CHUNK, CHUNK))
            g_last = g_col[CHUNK - 1:CHUNK, :]
            decay = jnp.where(causal, jnp.exp(jnp.where(causal, g_col - g_row, 0.0)), 0.0)
            exp_g = jnp.exp(g_col)

            kb = k * beta
            scores = _bdot_nt(jnp.concatenate([kb, q], axis=0), k)
            lowers.append(jnp.where(strict, scores[:CHUNK] * decay, 0.0))
            qks.append((scores[CHUNK:] * decay).astype(BF16))
            rhss.append(jnp.concatenate([v * beta, kb * exp_g], axis=-1))
            q_decs.append((q * exp_g).astype(BF16))
            k_decs.append((k * jnp.exp(g_last - g_col)).astype(BF16))
            last_decays.append(jnp.exp(g_last))
            if h % 3 == 2:
                tick()

    invs = _unit_lower_inverses_minus_identity(lowers, block_masks, tick)
    sols = [rhss[i] + _bdot(invs[i], rhss[i]) for i in range(len(invs))]

    states = [state_ref[h] for h in range(HEADS)]
    for c in range(n_chunks):
        rows = slice(c * CHUNK, (c + 1) * CHUNK)
        idx = [c * HEADS + h for h in range(HEADS)]
        wss = [_bdot(jnp.concatenate([sols[i][:, HEAD_DIM:].astype(BF16), q_decs[i]], axis=0),
                     states[h]) for h, i in enumerate(idx)]
        v_news = [sols[i][:, :HEAD_DIM] - wss[h][:CHUNK] for h, i in enumerate(idx)]
        v_news_b = [v.astype(BF16) for v in v_news]
        outs = [wss[h][CHUNK:] + jnp.dot(qks[i], v_news_b[h], preferred_element_type=F32)
                for h, i in enumerate(idx)]
        states = [states[h] * last_decays[i] + _bdot_tn(k_decs[i], v_news_b[h])
                  for h, i in enumerate(idx)]
        for h in range(HEADS):
            hcol = slice(h * HEAD_DIM, (h + 1) * HEAD_DIM)
            o = outs[h]
            o = o * lax.rsqrt(jnp.mean(o * o, axis=-1, keepdims=True) + EPS)
            y_buf[rows, POOL_WIDTH + h * HEAD_DIM:POOL_WIDTH + (h + 1) * HEAD_DIM] = (
                (o * o_gain_ref[...]) * z_buf[rows, hcol]).astype(BF16)
        tick()
    for h in range(HEADS):
        state_ref[h] = states[h]

    tick(D_FF // FFN_CHUNK + D_MODEL // FFN_CHUNK + 2)


def _decoder_block(x, mod, gain, w_in, conv_w, a_log, dt_bias, o_gain,
                   w_pool, pool_scale, w_out, ffn_gain, w_gate_up, w_down, final_gain):
    batch, seq, _ = x.shape
    ts = SEQ_TILE
    tiles_per_seq = seq // ts
    n_tiles = batch * tiles_per_seq

    def tile(t, lag):
        t = jnp.clip(t - lag, 0, n_tiles - 1)
        return t // tiles_per_seq, t % tiles_per_seq

    mix_tile = functools.partial(tile, lag=0)
    ffn_tile = functools.partial(tile, lag=1)

    def x_block(blk, t):
        return (*mix_tile(t), blk)

    def const(shape):
        return pl.BlockSpec(shape, lambda t: (0,) * len(shape), pipeline_mode=pl.Buffered(1))

    return pl.pallas_call(
        functools.partial(_block_kernel, tiles_per_seq),
        out_shape=jax.ShapeDtypeStruct(x.shape, F32),
        grid=(n_tiles + 1,),
        in_specs=[
            *[pl.BlockSpec((1, ts, LANES), functools.partial(x_block, blk))
              for blk in range(COL_BLOCKS)],
            pl.BlockSpec((1, N_MOD, D_MODEL), lambda t: (mix_tile(t)[0], 0, 0)),
            pl.BlockSpec((1, N_MOD, D_MODEL), lambda t: (ffn_tile(t)[0], 0, 0)),
            const((1, D_MODEL)),
            const((D_MODEL, IN_COLS_PADDED)),
            const((CONV_WIDTH, QKV_WIDTH)),
            const((1, LANES)),
            const((1, LANES)),
            const((1, HEAD_DIM)),
            const((POOL_WIDTH, POOL_WIDTH)),
            const((1, POOL_WIDTH)),
            const((D_MODEL, D_MODEL)),
            const((1, D_MODEL)),
            const((D_MODEL, 2 * D_FF)),
            const((D_FF, D_MODEL)),
            const((1, D_MODEL)),
        ],
        out_specs=pl.BlockSpec((1, ts, D_MODEL), lambda t: (*ffn_tile(t), 0)),
        scratch_shapes=[
            pltpu.VMEM((ts, QKV_WIDTH), F32),
            pltpu.VMEM((ts, QKV_WIDTH), F32),
            pltpu.VMEM((ts, GDN_WIDTH), F32),
            pltpu.VMEM((CONV_TAIL_GROUPS * SUBLANES, QKV_WIDTH), F32),
            pltpu.VMEM((POOL_LEVELS, POOL_TAIL_GROUPS * SUBLANES, POOL_WIDTH), F32),
            pltpu.VMEM((ts, D_MODEL), BF16),
            pltpu.VMEM((HEADS, HEAD_DIM, HEAD_DIM), F32),
            pltpu.VMEM((COL_BLOCKS, ts, LANES), F32),
            pltpu.VMEM((ts, D_FF), BF16),
            pltpu.VMEM((ts, D_MODEL), F32),
        ],
        compiler_params=pltpu.CompilerParams(
            dimension_semantics=("arbitrary",), vmem_limit_bytes=VMEM_LIMIT),
        name="decoder_block",
    )(*([x] * COL_BLOCKS), mod, mod, gain, w_in, conv_w, a_log, dt_bias, o_gain, w_pool,
      pool_scale, w_out, ffn_gain, w_gate_up, w_down, final_gain)


def _lane_row(values, offset):
    return jnp.zeros((1, LANES), F32).at[0, offset:offset + values.shape[0]].set(
        values.astype(F32))


def kernel(x, c, w_ada, b_ada, norm_mix_gain, w_in, conv_w, a_log, dt_bias, gdn_norm_gain,
           w_pool, pool_scale, w_out, norm_ffn_gain, w_gate_up, w_down, norm_final_gain):
    depth = w_ada.shape[0]
    assert depth == 1, "the block call applies the final norm, so only one layer is supported"
    batch = x.shape[0]
    l = 0
    mod = _modulation(c.astype(F32), w_ada[l], b_ada[l]).reshape(batch, N_MOD, D_MODEL)

    assert w_in.shape[-1] == IN_COLS
    w_in_padded = jnp.pad(w_in[l], ((0, 0), (0, IN_COLS_PADDED - IN_COLS))).astype(BF16)
    w_pool_bd = jnp.zeros((POOL_WIDTH, POOL_WIDTH), F32)
    for g in range(len(POOL_WINDOWS)):
        sl = slice(g * POOL_GROUP_DIM, (g + 1) * POOL_GROUP_DIM)
        w_pool_bd = w_pool_bd.at[sl, sl].set(w_pool[l, g])

    out = _decoder_block(
        x.astype(F32), mod, norm_mix_gain[l].reshape(1, D_MODEL).astype(F32),
        w_in_padded,
        conv_w[l].astype(F32),
        _lane_row(a_log[l], G_LANE),
        _lane_row(dt_bias[l], G_LANE),
        gdn_norm_gain[l].reshape(1, HEAD_DIM).astype(F32),
        w_pool_bd.astype(BF16), pool_scale[l].reshape(1, POOL_WIDTH).astype(F32),
        w_out[l].astype(BF16),
        norm_ffn_gain[l].reshape(1, D_MODEL).astype(F32),
        w_gate_up[l].astype(BF16),
        w_down[l].astype(BF16), norm_final_gain.reshape(1, D_MODEL).astype(F32))
    return out.astype(x.dtype)
```

```python
import functools

import jax
import jax.numpy as jnp
from jax import lax
from jax.experimental import pallas as pl
from jax.experimental.pallas import tpu as pltpu

D_MODEL = 1024
POOL_WIDTH = 256
POOL_GROUP_DIM = 64
POOL_WINDOWS = (2, 4, 8, 16)
HEADS = 6
HEAD_DIM = 128
GDN_WIDTH = HEADS * HEAD_DIM
QKV_WIDTH = 3 * GDN_WIDTH
CONV_WIDTH = 4
D_FF = 2816
N_MOD = 6
EPS = 1e-6

LANES = 128
SUBLANES = 8
CHUNK = 128
SEGMENT = CHUNK // SUBLANES
COL_BLOCKS = D_MODEL // LANES
BASE_BLOCK = 16
G_LANE = HEADS
O_QKV = POOL_WIDTH
O_Z = O_QKV + QKV_WIDTH
O_BA = O_Z + GDN_WIDTH
IN_COLS = O_BA + 2 * HEADS
BA_ROWS = 2 * SUBLANES
SEQ_TILE = 256
FFN_CHUNK = 256
VMEM_LIMIT = 56 * 1024 * 1024
CONV_TAIL_GROUPS = CONV_WIDTH - 1
POOL_LEVELS = len(POOL_WINDOWS)
POOL_TAIL_GROUPS = POOL_WINDOWS[-1] // 2

BF16 = jnp.bfloat16
F32 = jnp.float32


def _sigmoid(x):
    return 1.0 / (1.0 + jnp.exp(-x))


def _silu(x):
    return x * _sigmoid(x)


def _softplus(x):
    return jnp.maximum(x, 0.0) + jnp.log(1.0 + jnp.exp(-jnp.abs(x)))


def _bdot(a, b):
    return jnp.dot(a.astype(BF16), b.astype(BF16), preferred_element_type=F32)


def _bdot_nt(a, b):
    return lax.dot_general(a.astype(BF16), b.astype(BF16),
                           (((1,), (1,)), ((), ())), preferred_element_type=F32)


def _bdot_tn(a, b):
    return lax.dot_general(a.astype(BF16), b.astype(BF16),
                           (((0,), (0,)), ((), ())), preferred_element_type=F32)


def _mod_kernel(c_ref, w_ref, b_ref, o_ref):
    cond = _silu(c_ref[...])
    o_ref[...] = jnp.dot(cond, w_ref[...], preferred_element_type=F32,
                         precision=lax.Precision.HIGHEST) + b_ref[...]


def _modulation(c, w_ada, b_ada):
    batch = c.shape[0]
    n = w_ada.shape[1]
    tn = D_MODEL
    return pl.pallas_call(
        _mod_kernel,
        out_shape=jax.ShapeDtypeStruct((batch, n), F32),
        grid=(n // tn,),
        in_specs=[
            pl.BlockSpec((batch, D_MODEL), lambda j: (0, 0)),
            pl.BlockSpec((D_MODEL, tn), lambda j: (0, j)),
            pl.BlockSpec((1, tn), lambda j: (0, j)),
        ],
        out_specs=pl.BlockSpec((batch, tn), lambda j: (0, j)),
        compiler_params=pltpu.CompilerParams(
            dimension_semantics=("arbitrary",), vmem_limit_bytes=VMEM_LIMIT),
        name="adaln_modulation",
    )(c, w_ada, b_ada.reshape(1, n))


def _row_group(a, k):
    return a[k * SUBLANES:(k + 1) * SUBLANES]


def _token_of_row(row):
    return (row % SUBLANES) * SEGMENT + row // SUBLANES


def _wrapped_tail(cur, prev_tail, groups):
    first = lax.broadcasted_iota(jnp.int32, (SUBLANES, cur.shape[1]), 0) == 0
    out = []
    for k in range(groups):
        mine = _row_group(cur, SEGMENT - groups + k)
        theirs = _row_group(prev_tail, k)
        out.append(jnp.where(first, pltpu.roll(theirs, 1, 0), pltpu.roll(mine, 1, 0)))
    return jnp.concatenate(out, axis=0)


def _shift_tokens(cur, wrapped, j):
    n = j * SUBLANES
    return jnp.concatenate([wrapped[wrapped.shape[0] - n:], cur[:CHUNK - n]], axis=0)


def _chunk_cumsum(g):
    run = _row_group(g, 0)
    prefixes = [run]
    for k in range(1, SEGMENT):
        run = run + _row_group(g, k)
        prefixes.append(run)
    sub = lax.broadcasted_iota(jnp.int32, run.shape, 0)
    offset = jnp.where(sub >= 1, pltpu.roll(run, 1, 0), 0.0)
    shift = 1
    while shift < SUBLANES:
        offset = offset + jnp.where(sub >= shift, pltpu.roll(offset, shift, 0), 0.0)
        shift *= 2
    return jnp.concatenate([p + offset for p in prefixes], axis=0)


def _nested_block_masks(row, col):
    masks = []
    size = BASE_BLOCK
    inside = (row // size) == (col // size)
    masks.append(inside)
    while size < CHUNK:
        size *= 2
        outer = (row // size) == (col // size)
        masks.append(outer & ~inside)
        inside = outer
    return masks


def _unit_lower_inverses_minus_identity(lowers, block_masks, tick):
    n = range(len(lowers))
    invs = [-jnp.where(block_masks[0], l, 0.0) for l in lowers]
    powers = [inv.astype(BF16) for inv in invs]
    span = 2
    while span < BASE_BLOCK:
        powers_f = [jnp.dot(p, p, preferred_element_type=F32) for p in powers]
        powers = [p.astype(BF16) for p in powers_f]
        prods = [jnp.dot(invs[i].astype(BF16), powers[i], preferred_element_type=F32)
                 for i in n]
        invs = [invs[i] + powers_f[i] + prods[i] for i in n]
        span *= 2
        tick()
    for mask in block_masks[1:]:
        invs_b = [inv.astype(BF16) for inv in invs]
        offs = [jnp.where(mask, l, 0.0) for l in lowers]
        halves = [offs[i] + jnp.dot(invs_b[i], offs[i].astype(BF16),
                                    preferred_element_type=F32) for i in n]
        corrs = [jnp.dot(halves[i].astype(BF16), invs_b[i], preferred_element_type=F32)
                 for i in n]
        invs = [invs[i] - halves[i] - corrs[i] for i in n]
        tick()
    return invs


def _swiglu_stages(h_buf, act_buf, mod_ref, gain_ref, w_gate_up_ref, w_down_ref,
                   final_gain_ref, out_ref):
    h = jnp.concatenate([h_buf[blk] for blk in range(COL_BLOCKS)], axis=-1)
    mod = mod_ref[0]
    shift_f, scale_f, gate_f = mod[3:4], mod[4:5], mod[5:6]
    inv_rms = lax.rsqrt(jnp.mean(h * h, axis=-1, keepdims=True) + EPS)
    hn = ((h * inv_rms) * (gain_ref[...] * (1.0 + scale_f)) + shift_f).astype(BF16)
    yield
    for j in range(D_FF // FFN_CHUNK):
        cols = slice(j * FFN_CHUNK, (j + 1) * FFN_CHUNK)
        up_cols = slice(D_FF + j * FFN_CHUNK, D_FF + (j + 1) * FFN_CHUNK)
        gate = jnp.dot(hn, w_gate_up_ref[:, cols], preferred_element_type=F32)
        up = jnp.dot(hn, w_gate_up_ref[:, up_cols], preferred_element_type=F32)
        act_buf[:, cols] = (_silu(gate) * up).astype(BF16)
        yield
    acc = []
    for j in range(D_MODEL // FFN_CHUNK):
        acc.append(jnp.dot(act_buf[...], w_down_ref[:, j * FFN_CHUNK:(j + 1) * FFN_CHUNK],
                           preferred_element_type=F32))
        yield
    h2 = h + gate_f * jnp.concatenate(acc, axis=-1)
    inv_rms2 = lax.rsqrt(jnp.mean(h2 * h2, axis=-1, keepdims=True) + EPS)
    out_ref[0] = (h2 * inv_rms2) * final_gain_ref[...]


def _load_segment_major(x_refs, n_chunks):
    cols = []
    for x_ref in x_refs:
        groups = []
        for c in range(n_chunks):
            for r in range(SEGMENT):
                groups.append(x_ref[0, pl.ds(c * CHUNK + r, SUBLANES, stride=SEGMENT), :])
        cols.append(jnp.concatenate(groups, axis=0))
    return jnp.concatenate(cols, axis=-1)


def _block_kernel(tiles_per_seq, *refs):
    x_refs = refs[:COL_BLOCKS]
    (mod_mix_ref, mod_ffn_ref, gain_ref, w_in_ref, w_ba_t_ref,
     conv_w_ref, a_log_ref, dt_bias_ref, o_gain_ref, w_pool_ref,
     pool_scale_ref, w_out_ref, ffn_gain_ref, w_gate_up_ref,
     w_down_ref, final_gain_ref, out_ref,
     qkv_buf, act_buf, z_buf, conv_tail, pool_tail, y_buf, state_ref, h_buf,
     ffn_act_buf, x_prev) = refs[COL_BLOCKS:]
    ts = out_ref.shape[1]
    n_chunks = ts // CHUNK
    step = pl.program_id(0)
    seq_step = step % tiles_per_seq

    @pl.when(seq_step == 0)
    def _():
        conv_tail[...] = jnp.zeros_like(conv_tail)
        pool_tail[...] = jnp.zeros_like(pool_tail)
        state_ref[...] = jnp.zeros_like(state_ref)

    @pl.when(step == 0)
    def _():
        y_buf[...] = jnp.zeros_like(y_buf)
        x_prev[...] = jnp.zeros_like(x_prev)

    gate_prev = mod_ffn_ref[0][2:3]
    mix = jnp.dot(y_buf[...], w_out_ref[...], preferred_element_type=F32)
    h_prev = x_prev[...] + gate_prev * mix
    for blk in range(COL_BLOCKS):
        for c in range(n_chunks):
            for r in range(SEGMENT):
                lo = c * CHUNK + r * SUBLANES
                h_buf[blk, pl.ds(c * CHUNK + r, SUBLANES, stride=SEGMENT), :] = (
                    h_prev[lo:lo + SUBLANES, blk * LANES:(blk + 1) * LANES])

    swiglu = _swiglu_stages(h_buf, ffn_act_buf, mod_ffn_ref, ffn_gain_ref, w_gate_up_ref,
                            w_down_ref, final_gain_ref, out_ref)

    def tick(n=1):
        for _ in range(n):
            next(swiglu, None)

    x = _load_segment_major(x_refs, n_chunks)
    x_prev[...] = x
    mod = mod_mix_ref[0]
    shift_m, scale_m = mod[0:1], mod[1:2]
    inv_rms = lax.rsqrt(jnp.mean(x * x, axis=-1, keepdims=True) + EPS)
    hn = ((x * inv_rms) * (gain_ref[...] * (1.0 + scale_m)) + shift_m).astype(BF16)
    tick()

    u = jnp.dot(hn, w_in_ref[:, 0:O_QKV], preferred_element_type=F32)
    qkv_buf[...] = jnp.dot(hn, w_in_ref[:, O_QKV:O_Z], preferred_element_type=F32)
    ba_t = lax.dot_general(w_ba_t_ref[...], hn, (((1,), (1,)), ((), ())),
                           preferred_element_type=F32)
    ba_t = jnp.concatenate([ba_t, jnp.zeros((LANES - BA_ROWS, ts), F32)], axis=0)
    ba = jnp.concatenate([ba_t[:, c * CHUNK:(c + 1) * CHUNK].T for c in range(n_chunks)],
                         axis=0)
    z_buf[...] = _silu(jnp.dot(hn, w_in_ref[:, O_Z:O_BA], preferred_element_type=F32))

    lane = lax.broadcasted_iota(jnp.int32, (CHUNK, POOL_WIDTH), 1)
    window = jnp.full((CHUNK, POOL_WIDTH), POOL_WINDOWS[0], jnp.int32)
    for gi in range(1, POOL_LEVELS):
        window = jnp.where(lane >= gi * POOL_GROUP_DIM, POOL_WINDOWS[gi], window)
    chunk_token = _token_of_row(lax.broadcasted_iota(jnp.int32, (CHUNK, POOL_WIDTH), 0))
    pooled = []
    tail_lo = [(POOL_TAIL_GROUPS - w // 2) * SUBLANES for w in POOL_WINDOWS]
    prev_tails = [pool_tail[level, tail_lo[level]:, :] for level in range(POOL_LEVELS)]
    for c in range(n_chunks):
        u_c = u[c * CHUNK:(c + 1) * CHUNK]
        sums = u_c
        total = None
        for level in range(POOL_LEVELS):
            j = POOL_WINDOWS[level] // 2
            tail = prev_tails[level]
            prev_tails[level] = sums[CHUNK - j * SUBLANES:]
            sums = sums + _shift_tokens(sums, _wrapped_tail(sums, tail, j), j)
            total = sums if total is None else jnp.where(
                lane >= level * POOL_GROUP_DIM, sums, total)
        pos = chunk_token + (seq_step * ts + c * CHUNK + 1)
        count = jnp.minimum(pos, window).astype(F32)
        pooled.append(total / count - u_c)
    for level in range(POOL_LEVELS):
        pool_tail[level, tail_lo[level]:, :] = prev_tails[level]
    pooled = jnp.concatenate(pooled, axis=0)
    mixed = _bdot(pooled, w_pool_ref[...])
    mixed = mixed * lax.rsqrt(jnp.mean(mixed * mixed, axis=-1, keepdims=True) + EPS)
    y_buf[:, 0:POOL_WIDTH] = (mixed * pool_scale_ref[...]).astype(BF16)
    tick(2)

    tail_rows = CONV_TAIL_GROUPS * SUBLANES
    for blk in range(QKV_WIDTH // GDN_WIDTH):
        cols = slice(blk * GDN_WIDTH, (blk + 1) * GDN_WIDTH)
        prev_tail = conv_tail[:, cols]
        for c in range(n_chunks):
            rows = slice(c * CHUNK, (c + 1) * CHUNK)
            cur = qkv_buf[rows, cols]
            wrapped = _wrapped_tail(cur, prev_tail, CONV_TAIL_GROUPS)
            conv = conv_w_ref[CONV_WIDTH - 1:CONV_WIDTH, cols] * cur
            for j in range(1, CONV_WIDTH):
                conv = conv + (conv_w_ref[CONV_WIDTH - 1 - j:CONV_WIDTH - j, cols]
                               * _shift_tokens(cur, wrapped, j))
            act_buf[rows, cols] = _silu(conv)
            prev_tail = cur[CHUNK - tail_rows:]
        conv_tail[:, cols] = prev_tail
        tick(3)

    beta_all = _sigmoid(ba)
    g_all = -jnp.exp(a_log_ref[...]) * _softplus(ba + dt_bias_ref[...])

    row = _token_of_row(lax.broadcasted_iota(jnp.int32, (CHUNK, CHUNK), 0))
    col = _token_of_row(lax.broadcasted_iota(jnp.int32, (CHUNK, CHUNK), 1))
    causal = row >= col
    strict = row > col
    block_masks = _nested_block_masks(row, col)

    lowers, qks, rhss, q_decs, k_decs, last_decays = [], [], [], [], [], []
    for c in range(n_chunks):
        rows = slice(c * CHUNK, (c + 1) * CHUNK)
        g_cum = _chunk_cumsum(g_all[rows])
        g_cum_t = g_cum.T
        beta_c = beta_all[rows]
        for h in range(HEADS):
            hcol = slice(h * HEAD_DIM, (h + 1) * HEAD_DIM)
            q = act_buf[rows, hcol]
            k = act_buf[rows, GDN_WIDTH + h * HEAD_DIM:GDN_WIDTH + (h + 1) * HEAD_DIM]
            v = act_buf[rows, 2 * GDN_WIDTH + h * HEAD_DIM:2 * GDN_WIDTH + (h + 1) * HEAD_DIM]
            q = q * (lax.rsqrt(jnp.sum(q * q, axis=-1, keepdims=True) + EPS)
                     * (HEAD_DIM ** -0.5))
            k = k * lax.rsqrt(jnp.sum(k * k, axis=-1, keepdims=True) + EPS)

            beta = jnp.broadcast_to(beta_c[:, h:h + 1], (CHUNK, HEAD_DIM))
            g_col = jnp.broadcast_to(g_cum[:, G_LANE + h:G_LANE + h + 1], (CHUNK, CHUNK))
            g_row = jnp.broadcast_to(g_cum_t[G_LANE + h:G_LANE + h + 1, :], (CHUNK, CHUNK))
            g_last = g_col[CHUNK - 1:CHUNK, :]
            decay = jnp.where(causal, jnp.exp(jnp.where(causal, g_col - g_row, 0.0)), 0.0)
            exp_g = jnp.exp(g_col)

            kb = k * beta
            scores = _bdot_nt(jnp.concatenate([kb, q], axis=0), k)
            lowers.append(jnp.where(strict, scores[:CHUNK] * decay, 0.0))
            qks.append((scores[CHUNK:] * decay).astype(BF16))
            rhss.append(jnp.concatenate([v * beta, kb * exp_g], axis=-1))
            q_decs.append((q * exp_g).astype(BF16))
            k_decs.append((k * jnp.exp(g_last - g_col)).astype(BF16))
            last_decays.append(jnp.exp(g_last))
            if h % 3 == 2:
                tick()

    invs = _unit_lower_inverses_minus_identity(lowers, block_masks, tick)
    sols = [rhss[i] + _bdot(invs[i], rhss[i]) for i in range(len(invs))]

    states = [state_ref[h] for h in range(HEADS)]
    for c in range(n_chunks):
        rows = slice(c * CHUNK, (c + 1) * CHUNK)
        idx = [c * HEADS + h for h in range(HEADS)]
        wss = [_bdot(jnp.concatenate([sols[i][:, HEAD_DIM:].astype(BF16), q_decs[i]], axis=0),
                     states[h]) for h, i in enumerate(idx)]
        v_news = [sols[i][:, :HEAD_DIM] - wss[h][:CHUNK] for h, i in enumerate(idx)]
        v_news_b = [v.astype(BF16) for v in v_news]
        outs = [wss[h][CHUNK:] + jnp.dot(qks[i], v_news_b[h], preferred_element_type=F32)
                for h, i in enumerate(idx)]
        states = [states[h] * last_decays[i] + _bdot_tn(k_decs[i], v_news_b[h])
                  for h, i in enumerate(idx)]
        for h in range(HEADS):
            hcol = slice(h * HEAD_DIM, (h + 1) * HEAD_DIM)
            o = outs[h]
            o = o * lax.rsqrt(jnp.mean(o * o, axis=-1, keepdims=True) + EPS)
            y_buf[rows, POOL_WIDTH + h * HEAD_DIM:POOL_WIDTH + (h + 1) * HEAD_DIM] = (
                (o * o_gain_ref[...]) * z_buf[rows, hcol]).astype(BF16)
        tick()
    for h in range(HEADS):
        state_ref[h] = states[h]

    tick(D_FF // FFN_CHUNK + D_MODEL // FFN_CHUNK + 2)


def _decoder_block(x, mod, gain, w_in, w_ba_t, conv_w, a_log, dt_bias, o_gain,
                   w_pool, pool_scale, w_out, ffn_gain, w_gate_up, w_down, final_gain):
    batch, seq, _ = x.shape
    ts = SEQ_TILE
    tiles_per_seq = seq // ts
    n_tiles = batch * tiles_per_seq

    def tile(t, lag):
        t = jnp.clip(t - lag, 0, n_tiles - 1)
        return t // tiles_per_seq, t % tiles_per_seq

    mix_tile = functools.partial(tile, lag=0)
    ffn_tile = functools.partial(tile, lag=1)

    def x_block(blk, t):
        return (*mix_tile(t), blk)

    def const(shape):
        return pl.BlockSpec(shape, lambda t: (0,) * len(shape), pipeline_mode=pl.Buffered(1))

    return pl.pallas_call(
        functools.partial(_block_kernel, tiles_per_seq),
        out_shape=jax.ShapeDtypeStruct(x.shape, F32),
        grid=(n_tiles + 1,),
        in_specs=[
            *[pl.BlockSpec((1, ts, LANES), functools.partial(x_block, blk))
              for blk in range(COL_BLOCKS)],
            pl.BlockSpec((1, N_MOD, D_MODEL), lambda t: (mix_tile(t)[0], 0, 0)),
            pl.BlockSpec((1, N_MOD, D_MODEL), lambda t: (ffn_tile(t)[0], 0, 0)),
            const((1, D_MODEL)),
            const((D_MODEL, O_BA)),
            const((BA_ROWS, D_MODEL)),
            const((CONV_WIDTH, QKV_WIDTH)),
            const((1, LANES)),
            const((1, LANES)),
            const((1, HEAD_DIM)),
            const((POOL_WIDTH, POOL_WIDTH)),
            const((1, POOL_WIDTH)),
            const((D_MODEL, D_MODEL)),
            const((1, D_MODEL)),
            const((D_MODEL, 2 * D_FF)),
            const((D_FF, D_MODEL)),
            const((1, D_MODEL)),
        ],
        out_specs=pl.BlockSpec((1, ts, D_MODEL), lambda t: (*ffn_tile(t), 0)),
        scratch_shapes=[
            pltpu.VMEM((ts, QKV_WIDTH), F32),
            pltpu.VMEM((ts, QKV_WIDTH), F32),
            pltpu.VMEM((ts, GDN_WIDTH), F32),
            pltpu.VMEM((CONV_TAIL_GROUPS * SUBLANES, QKV_WIDTH), F32),
            pltpu.VMEM((POOL_LEVELS, POOL_TAIL_GROUPS * SUBLANES, POOL_WIDTH), F32),
            pltpu.VMEM((ts, D_MODEL), BF16),
            pltpu.VMEM((HEADS, HEAD_DIM, HEAD_DIM), F32),
            pltpu.VMEM((COL_BLOCKS, ts, LANES), F32),
            pltpu.VMEM((ts, D_FF), BF16),
            pltpu.VMEM((ts, D_MODEL), F32),
        ],
        compiler_params=pltpu.CompilerParams(
            dimension_semantics=("arbitrary",), vmem_limit_bytes=VMEM_LIMIT),
        name="decoder_block",
    )(*([x] * COL_BLOCKS), mod, mod, gain, w_in, w_ba_t, conv_w, a_log, dt_bias, o_gain, w_pool,
      pool_scale, w_out, ffn_gain, w_gate_up, w_down, final_gain)


def _lane_row(values, offset):
    return jnp.zeros((1, LANES), F32).at[0, offset:offset + values.shape[0]].set(
        values.astype(F32))


def kernel(x, c, w_ada, b_ada, norm_mix_gain, w_in, conv_w, a_log, dt_bias, gdn_norm_gain,
           w_pool, pool_scale, w_out, norm_ffn_gain, w_gate_up, w_down, norm_final_gain):
    depth = w_ada.shape[0]
    assert depth == 1, "the block call applies the final norm, so only one layer is supported"
    batch = x.shape[0]
    l = 0
    mod = _modulation(c.astype(F32), w_ada[l], b_ada[l]).reshape(batch, N_MOD, D_MODEL)

    assert w_in.shape[-1] == IN_COLS
    w_main = w_in[l][:, :O_BA].astype(BF16)
    w_ba_t = jnp.pad(w_in[l][:, O_BA:IN_COLS].T,
                     ((0, BA_ROWS - 2 * HEADS), (0, 0))).astype(BF16)
    w_pool_bd = jnp.zeros((POOL_WIDTH, POOL_WIDTH), F32)
    for g in range(len(POOL_WINDOWS)):
        sl = slice(g * POOL_GROUP_DIM, (g + 1) * POOL_GROUP_DIM)
        w_pool_bd = w_pool_bd.at[sl, sl].set(w_pool[l, g])

    out = _decoder_block(
        x.astype(F32), mod, norm_mix_gain[l].reshape(1, D_MODEL).astype(F32),
        w_main, w_ba_t,
        conv_w[l].astype(F32),
        _lane_row(a_log[l], G_LANE),
        _lane_row(dt_bias[l], G_LANE),
        gdn_norm_gain[l].reshape(1, HEAD_DIM).astype(F32),
        w_pool_bd.astype(BF16), pool_scale[l].reshape(1, POOL_WIDTH).astype(F32),
        w_out[l].astype(BF16),
        norm_ffn_gain[l].reshape(1, D_MODEL).astype(F32),
        w_gate_up[l].astype(BF16),
        w_down[l].astype(BF16), norm_final_gain.reshape(1, D_MODEL).astype(F32))
    return out.astype(x.dtype)
```

```python
import functools

import jax
import jax.numpy as jnp
from jax import lax
from jax.experimental import pallas as pl
from jax.experimental.pallas import tpu as pltpu

D_MODEL = 1024
POOL_WIDTH = 256
POOL_GROUP_DIM = 64
POOL_WINDOWS = (2, 4, 8, 16)
HEADS = 6
HEAD_DIM = 128
GDN_WIDTH = HEADS * HEAD_DIM
QKV_WIDTH = 3 * GDN_WIDTH
CONV_WIDTH = 4
D_FF = 2816
N_MOD = 6
EPS = 1e-6

LANES = 128
SUBLANES = 8
CHUNK = 128
SEGMENT = CHUNK // SUBLANES
COL_BLOCKS = D_MODEL // LANES
BASE_BLOCK = 16
G_LANE = HEADS
O_QKV = POOL_WIDTH
O_Z = O_QKV + QKV_WIDTH
O_BA = O_Z + GDN_WIDTH
IN_COLS = O_BA + 2 * HEADS
SEQ_TILE = 256
FFN_CHUNK = 256
VMEM_LIMIT = 56 * 1024 * 1024
CONV_TAIL_GROUPS = CONV_WIDTH - 1
POOL_LEVELS = len(POOL_WINDOWS)
POOL_TAIL_GROUPS = POOL_WINDOWS[-1] // 2

BF16 = jnp.bfloat16
F32 = jnp.float32


def _sigmoid(x):
    return 1.0 / (1.0 + jnp.exp(-x))


def _silu(x):
    return x * _sigmoid(x)


def _softplus(x):
    return jnp.maximum(x, 0.0) + jnp.log(1.0 + jnp.exp(-jnp.abs(x)))


def _bdot(a, b):
    return jnp.dot(a.astype(BF16), b.astype(BF16), preferred_element_type=F32)


def _bdot_nt(a, b):
    return lax.dot_general(a.astype(BF16), b.astype(BF16),
                           (((1,), (1,)), ((), ())), preferred_element_type=F32)


def _bdot_tn(a, b):
    return lax.dot_general(a.astype(BF16), b.astype(BF16),
                           (((0,), (0,)), ((), ())), preferred_element_type=F32)


def _mod_kernel(c_ref, w_ref, b_ref, o_ref):
    cond = _silu(c_ref[...])
    o_ref[...] = jnp.dot(cond, w_ref[...], preferred_element_type=F32,
                         precision=lax.Precision.HIGHEST) + b_ref[...]


def _modulation(c, w_ada, b_ada):
    batch = c.shape[0]
    n = w_ada.shape[1]
    tn = D_MODEL
    return pl.pallas_call(
        _mod_kernel,
        out_shape=jax.ShapeDtypeStruct((batch, n), F32),
        grid=(n // tn,),
        in_specs=[
            pl.BlockSpec((batch, D_MODEL), lambda j: (0, 0)),
            pl.BlockSpec((D_MODEL, tn), lambda j: (0, j)),
            pl.BlockSpec((1, tn), lambda j: (0, j)),
        ],
        out_specs=pl.BlockSpec((batch, tn), lambda j: (0, j)),
        compiler_params=pltpu.CompilerParams(
            dimension_semantics=("arbitrary",), vmem_limit_bytes=VMEM_LIMIT),
        name="adaln_modulation",
    )(c, w_ada, b_ada.reshape(1, n))


def _row_group(a, k):
    return a[k * SUBLANES:(k + 1) * SUBLANES]


def _token_of_row(row):
    return (row % SUBLANES) * SEGMENT + row // SUBLANES


def _wrapped_tail(cur, prev_tail, groups):
    first = lax.broadcasted_iota(jnp.int32, (SUBLANES, cur.shape[1]), 0) == 0
    out = []
    for k in range(groups):
        mine = _row_group(cur, SEGMENT - groups + k)
        theirs = _row_group(prev_tail, k)
        out.append(jnp.where(first, pltpu.roll(theirs, 1, 0), pltpu.roll(mine, 1, 0)))
    return jnp.concatenate(out, axis=0)


def _shift_tokens(cur, wrapped, j):
    n = j * SUBLANES
    return jnp.concatenate([wrapped[wrapped.shape[0] - n:], cur[:CHUNK - n]], axis=0)


def _chunk_cumsum(g):
    run = _row_group(g, 0)
    prefixes = [run]
    for k in range(1, SEGMENT):
        run = run + _row_group(g, k)
        prefixes.append(run)
    sub = lax.broadcasted_iota(jnp.int32, run.shape, 0)
    offset = jnp.where(sub >= 1, pltpu.roll(run, 1, 0), 0.0)
    shift = 1
    while shift < SUBLANES:
        offset = offset + jnp.where(sub >= shift, pltpu.roll(offset, shift, 0), 0.0)
        shift *= 2
    return jnp.concatenate([p + offset for p in prefixes], axis=0)


def _nested_block_masks(row, col):
    masks = []
    size = BASE_BLOCK
    inside = (row // size) == (col // size)
    masks.append(inside)
    while size < CHUNK:
        size *= 2
        outer = (row // size) == (col // size)
        masks.append(outer & ~inside)
        inside = outer
    return masks


def _unit_lower_inverses_minus_identity(lowers, block_masks, tick):
    n = range(len(lowers))
    invs = [-jnp.where(block_masks[0], l, 0.0) for l in lowers]
    powers = [inv.astype(BF16) for inv in invs]
    span = 2
    while span < BASE_BLOCK:
        powers_f = [jnp.dot(p, p, preferred_element_type=F32) for p in powers]
        powers = [p.astype(BF16) for p in powers_f]
        prods = [jnp.dot(invs[i].astype(BF16), powers[i], preferred_element_type=F32)
                 for i in n]
        invs = [invs[i] + powers_f[i] + prods[i] for i in n]
        span *= 2
        tick()
    for mask in block_masks[1:]:
        invs_b = [inv.astype(BF16) for inv in invs]
        offs = [jnp.where(mask, l, 0.0) for l in lowers]
        halves = [offs[i] + jnp.dot(invs_b[i], offs[i].astype(BF16),
                                    preferred_element_type=F32) for i in n]
        corrs = [jnp.dot(halves[i].astype(BF16), invs_b[i], preferred_element_type=F32)
                 for i in n]
        invs = [invs[i] - halves[i] - corrs[i] for i in n]
        tick()
    return invs


def _swiglu_stages(h_buf, act_buf, mod_ref, gain_ref, w_gate_up_ref, w_down_ref,
                   final_gain_ref, out_ref):
    h = jnp.concatenate([h_buf[blk] for blk in range(COL_BLOCKS)], axis=-1)
    mod = mod_ref[0]
    shift_f, scale_f, gate_f = mod[3:4], mod[4:5], mod[5:6]
    inv_rms = lax.rsqrt(jnp.mean(h * h, axis=-1, keepdims=True) + EPS)
    hn = ((h * inv_rms) * (gain_ref[...] * (1.0 + scale_f)) + shift_f).astype(BF16)
    yield
    for j in range(D_FF // FFN_CHUNK):
        cols = slice(j * FFN_CHUNK, (j + 1) * FFN_CHUNK)
        up_cols = slice(D_FF + j * FFN_CHUNK, D_FF + (j + 1) * FFN_CHUNK)
        gate = jnp.dot(hn, w_gate_up_ref[:, cols], preferred_element_type=F32)
        up = jnp.dot(hn, w_gate_up_ref[:, up_cols], preferred_element_type=F32)
        act_buf[:, cols] = (_silu(gate) * up).astype(BF16)
        yield
    acc = []
    for j in range(D_MODEL // FFN_CHUNK):
        acc.append(jnp.dot(act_buf[...], w_down_ref[:, j * FFN_CHUNK:(j + 1) * FFN_CHUNK],
                           preferred_element_type=F32))
        yield
    h2 = h + gate_f * jnp.concatenate(acc, axis=-1)
    inv_rms2 = lax.rsqrt(jnp.mean(h2 * h2, axis=-1, keepdims=True) + EPS)
    out_ref[0] = (h2 * inv_rms2) * final_gain_ref[...]


def _load_segment_major(x_refs, n_chunks):
    cols = []
    for x_ref in x_refs:
        groups = []
        for c in range(n_chunks):
            for r in range(SEGMENT):
                groups.append(x_ref[0, pl.ds(c * CHUNK + r, SUBLANES, stride=SEGMENT), :])
        cols.append(jnp.concatenate(groups, axis=0))
    return jnp.concatenate(cols, axis=-1)


def _block_kernel(tiles_per_seq, *refs):
    x_refs = refs[:COL_BLOCKS]
    (mod_mix_ref, mod_ffn_ref, gain_ref, w_in_ref, w_ba_ref,
     conv_w_ref, a_log_ref, dt_bias_ref, o_gain_ref, w_pool_ref,
     pool_scale_ref, w_out_ref, ffn_gain_ref, w_gate_up_ref,
     w_down_ref, final_gain_ref, out_ref,
     qkv_buf, act_buf, z_buf, conv_tail, pool_tail, y_buf, state_ref, h_buf,
     ffn_act_buf, x_prev) = refs[COL_BLOCKS:]
    ts = out_ref.shape[1]
    n_chunks = ts // CHUNK
    step = pl.program_id(0)
    seq_step = step % tiles_per_seq

    @pl.when(seq_step == 0)
    def _():
        conv_tail[...] = jnp.zeros_like(conv_tail)
        pool_tail[...] = jnp.zeros_like(pool_tail)
        state_ref[...] = jnp.zeros_like(state_ref)

    @pl.when(step == 0)
    def _():
        y_buf[...] = jnp.zeros_like(y_buf)
        x_prev[...] = jnp.zeros_like(x_prev)

    gate_prev = mod_ffn_ref[0][2:3]
    mix = jnp.dot(y_buf[...], w_out_ref[...], preferred_element_type=F32)
    h_prev = x_prev[...] + gate_prev * mix
    for blk in range(COL_BLOCKS):
        for c in range(n_chunks):
            for r in range(SEGMENT):
                lo = c * CHUNK + r * SUBLANES
                h_buf[blk, pl.ds(c * CHUNK + r, SUBLANES, stride=SEGMENT), :] = (
                    h_prev[lo:lo + SUBLANES, blk * LANES:(blk + 1) * LANES])

    swiglu = _swiglu_stages(h_buf, ffn_act_buf, mod_ffn_ref, ffn_gain_ref, w_gate_up_ref,
                            w_down_ref, final_gain_ref, out_ref)

    def tick(n=1):
        for _ in range(n):
            next(swiglu, None)

    x = _load_segment_major(x_refs, n_chunks)
    x_prev[...] = x
    mod = mod_mix_ref[0]
    shift_m, scale_m = mod[0:1], mod[1:2]
    inv_rms = lax.rsqrt(jnp.mean(x * x, axis=-1, keepdims=True) + EPS)
    hn = ((x * inv_rms) * (gain_ref[...] * (1.0 + scale_m)) + shift_m).astype(BF16)
    tick()

    u = jnp.dot(hn, w_in_ref[:, 0:O_QKV], preferred_element_type=F32)
    qkv_buf[...] = jnp.dot(hn, w_in_ref[:, O_QKV:O_Z], preferred_element_type=F32)
    ba = jnp.dot(hn, w_ba_ref[...], preferred_element_type=F32)
    z_buf[...] = _silu(jnp.dot(hn, w_in_ref[:, O_Z:O_BA], preferred_element_type=F32))

    lane = lax.broadcasted_iota(jnp.int32, (CHUNK, POOL_WIDTH), 1)
    window = jnp.full((CHUNK, POOL_WIDTH), POOL_WINDOWS[0], jnp.int32)
    for gi in range(1, POOL_LEVELS):
        window = jnp.where(lane >= gi * POOL_GROUP_DIM, POOL_WINDOWS[gi], window)
    chunk_token = _token_of_row(lax.broadcasted_iota(jnp.int32, (CHUNK, POOL_WIDTH), 0))
    pooled = []
    tail_lo = [(POOL_TAIL_GROUPS - w // 2) * SUBLANES for w in POOL_WINDOWS]
    prev_tails = [pool_tail[level, tail_lo[level]:, :] for level in range(POOL_LEVELS)]
    for c in range(n_chunks):
        u_c = u[c * CHUNK:(c + 1) * CHUNK]
        sums = u_c
        total = None
        for level in range(POOL_LEVELS):
            j = POOL_WINDOWS[level] // 2
            tail = prev_tails[level]
            prev_tails[level] = sums[CHUNK - j * SUBLANES:]
            sums = sums + _shift_tokens(sums, _wrapped_tail(sums, tail, j), j)
            total = sums if total is None else jnp.where(
                lane >= level * POOL_GROUP_DIM, sums, total)
        pos = chunk_token + (seq_step * ts + c * CHUNK + 1)
        count = jnp.minimum(pos, window).astype(F32)
        pooled.append(total / count - u_c)
    for level in range(POOL_LEVELS):
        pool_tail[level, tail_lo[level]:, :] = prev_tails[level]
    pooled = jnp.concatenate(pooled, axis=0)
    mixed = _bdot(pooled, w_pool_ref[...])
    mixed = mixed * lax.rsqrt(jnp.mean(mixed * mixed, axis=-1, keepdims=True) + EPS)
    y_buf[:, 0:POOL_WIDTH] = (mixed * pool_scale_ref[...]).astype(BF16)
    tick(2)

    tail_rows = CONV_TAIL_GROUPS * SUBLANES
    for blk in range(QKV_WIDTH // GDN_WIDTH):
        cols = slice(blk * GDN_WIDTH, (blk + 1) * GDN_WIDTH)
        prev_tail = conv_tail[:, cols]
        for c in range(n_chunks):
            rows = slice(c * CHUNK, (c + 1) * CHUNK)
            cur = qkv_buf[rows, cols]
            wrapped = _wrapped_tail(cur, prev_tail, CONV_TAIL_GROUPS)
            conv = conv_w_ref[CONV_WIDTH - 1:CONV_WIDTH, cols] * cur
            for j in range(1, CONV_WIDTH):
                conv = conv + (conv_w_ref[CONV_WIDTH - 1 - j:CONV_WIDTH - j, cols]
                               * _shift_tokens(cur, wrapped, j))
            act_buf[rows, cols] = _silu(conv)
            prev_tail = cur[CHUNK - tail_rows:]
        conv_tail[:, cols] = prev_tail
        tick(3)

    beta_all = _sigmoid(ba)
    g_all = -jnp.exp(a_log_ref[...]) * _softplus(ba + dt_bias_ref[...])

    row = _token_of_row(lax.broadcasted_iota(jnp.int32, (CHUNK, CHUNK), 0))
    col = _token_of_row(lax.broadcasted_iota(jnp.int32, (CHUNK, CHUNK), 1))
    causal = row >= col
    strict = row > col
    block_masks = _nested_block_masks(row, col)

    lowers, qks, rhss, q_decs, k_decs, last_decays = [], [], [], [], [], []
    for c in range(n_chunks):
        rows = slice(c * CHUNK, (c + 1) * CHUNK)
        g_cum = _chunk_cumsum(g_all[rows])
        g_cum_t = g_cum.T
        beta_c = beta_all[rows]
        for h in range(HEADS):
            hcol = slice(h * HEAD_DIM, (h + 1) * HEAD_DIM)
            q = act_buf[rows, hcol]
            k = act_buf[rows, GDN_WIDTH + h * HEAD_DIM:GDN_WIDTH + (h + 1) * HEAD_DIM]
            v = act_buf[rows, 2 * GDN_WIDTH + h * HEAD_DIM:2 * GDN_WIDTH + (h + 1) * HEAD_DIM]
            q = q * (lax.rsqrt(jnp.sum(q * q, axis=-1, keepdims=True) + EPS)
                     * (HEAD_DIM ** -0.5))
            k = k * lax.rsqrt(jnp.sum(k * k, axis=-1, keepdims=True) + EPS)

            beta = jnp.broadcast_to(beta_c[:, h:h + 1], (CHUNK, HEAD_DIM))
            g_col = jnp.broadcast_to(g_cum[:, G_LANE + h:G_LANE + h + 1], (CHUNK, CHUNK))
            g_row = jnp.broadcast_to(g_cum_t[G_LANE + h:G_LANE + h + 1, :], (CHUNK, CHUNK))
            g_last = g_col[CHUNK - 1:CHUNK, :]
            decay = jnp.where(causal, jnp.exp(jnp.where(causal, g_col - g_row, 0.0)), 0.0)
            exp_g = jnp.exp(g_col)

            kb = k * beta
            scores = _bdot_nt(jnp.concatenate([kb, q], axis=0), k)
            lowers.append(jnp.where(strict, scores[:CHUNK] * decay, 0.0))
            qks.append((scores[CHUNK:] * decay).astype(BF16))
            rhss.append(jnp.concatenate([v * beta, kb * exp_g], axis=-1))
            q_decs.append((q * exp_g).astype(BF16))
            k_decs.append((k * jnp.exp(g_last - g_col)).astype(BF16))
            last_decays.append(jnp.exp(g_last))
            if h % 3 == 2:
                tick()

    invs = _unit_lower_inverses_minus_identity(lowers, block_masks, tick)
    sols = [rhss[i] + _bdot(invs[i], rhss[i]) for i in range(len(invs))]

    states = [state_ref[h] for h in range(HEADS)]
    for c in range(n_chunks):
        rows = slice(c * CHUNK, (c + 1) * CHUNK)
        idx = [c * HEADS + h for h in range(HEADS)]
        wss = [_bdot(jnp.concatenate([sols[i][:, HEAD_DIM:].astype(BF16), q_decs[i]], axis=0),
                     states[h]) for h, i in enumerate(idx)]
        v_news = [sols[i][:, :HEAD_DIM] - wss[h][:CHUNK] for h, i in enumerate(idx)]
        v_news_b = [v.astype(BF16) for v in v_news]
        outs = [wss[h][CHUNK:] + jnp.dot(qks[i], v_news_b[h], preferred_element_type=F32)
                for h, i in enumerate(idx)]
        states = [states[h] * last_decays[i] + _bdot_tn(k_decs[i], v_news_b[h])
                  for h, i in enumerate(idx)]
        for h in range(HEADS):
            hcol = slice(h * HEAD_DIM, (h + 1) * HEAD_DIM)
            o = outs[h]
            o = o * lax.rsqrt(jnp.mean(o * o, axis=-1, keepdims=True) + EPS)
            y_buf[rows, POOL_WIDTH + h * HEAD_DIM:POOL_WIDTH + (h + 1) * HEAD_DIM] = (
                (o * o_gain_ref[...]) * z_buf[rows, hcol]).astype(BF16)
        tick()
    for h in range(HEADS):
        state_ref[h] = states[h]

    tick(D_FF // FFN_CHUNK + D_MODEL // FFN_CHUNK + 2)


def _decoder_block(x, mod, gain, w_in, w_ba, conv_w, a_log, dt_bias, o_gain,
                   w_pool, pool_scale, w_out, ffn_gain, w_gate_up, w_down, final_gain):
    batch, seq, _ = x.shape
    ts = SEQ_TILE
    tiles_per_seq = seq // ts
    n_tiles = batch * tiles_per_seq

    def tile(t, lag):
        t = jnp.clip(t - lag, 0, n_tiles - 1)
        return t // tiles_per_seq, t % tiles_per_seq

    mix_tile = functools.partial(tile, lag=0)
    ffn_tile = functools.partial(tile, lag=1)

    def x_block(blk, t):
        return (*mix_tile(t), blk)

    def const(shape):
        return pl.BlockSpec(shape, lambda t: (0,) * len(shape), pipeline_mode=pl.Buffered(1))

    return pl.pallas_call(
        functools.partial(_block_kernel, tiles_per_seq),
        out_shape=jax.ShapeDtypeStruct(x.shape, F32),
        grid=(n_tiles + 1,),
        in_specs=[
            *[pl.BlockSpec((1, ts, LANES), functools.partial(x_block, blk))
              for blk in range(COL_BLOCKS)],
            pl.BlockSpec((1, N_MOD, D_MODEL), lambda t: (mix_tile(t)[0], 0, 0)),
            pl.BlockSpec((1, N_MOD, D_MODEL), lambda t: (ffn_tile(t)[0], 0, 0)),
            const((1, D_MODEL)),
            const((D_MODEL, O_BA)),
            const((D_MODEL, LANES)),
            const((CONV_WIDTH, QKV_WIDTH)),
            const((1, LANES)),
            const((1, LANES)),
            const((1, HEAD_DIM)),
            const((POOL_WIDTH, POOL_WIDTH)),
            const((1, POOL_WIDTH)),
            const((D_MODEL, D_MODEL)),
            const((1, D_MODEL)),
            const((D_MODEL, 2 * D_FF)),
            const((D_FF, D_MODEL)),
            const((1, D_MODEL)),
        ],
        out_specs=pl.BlockSpec((1, ts, D_MODEL), lambda t: (*ffn_tile(t), 0)),
        scratch_shapes=[
            pltpu.VMEM((ts, QKV_WIDTH), F32),
            pltpu.VMEM((ts, QKV_WIDTH), F32),
            pltpu.VMEM((ts, GDN_WIDTH), F32),
            pltpu.VMEM((CONV_TAIL_GROUPS * SUBLANES, QKV_WIDTH), F32),
            pltpu.VMEM((POOL_LEVELS, POOL_TAIL_GROUPS * SUBLANES, POOL_WIDTH), F32),
            pltpu.VMEM((ts, D_MODEL), BF16),
            pltpu.VMEM((HEADS, HEAD_DIM, HEAD_DIM), F32),
            pltpu.VMEM((COL_BLOCKS, ts, LANES), F32),
            pltpu.VMEM((ts, D_FF), BF16),
            pltpu.VMEM((ts, D_MODEL), F32),
        ],
        compiler_params=pltpu.CompilerParams(
            dimension_semantics=("arbitrary",), vmem_limit_bytes=VMEM_LIMIT),
        name="decoder_block",
    )(*([x] * COL_BLOCKS), mod, mod, gain, w_in, w_ba, conv_w, a_log, dt_bias, o_gain, w_pool,
      pool_scale, w_out, ffn_gain, w_gate_up, w_down, final_gain)


def _lane_row(values, offset):
    return jnp.zeros((1, LANES), F32).at[0, offset:offset + values.shape[0]].set(
        values.astype(F32))


def kernel(x, c, w_ada, b_ada, norm_mix_gain, w_in, conv_w, a_log, dt_bias, gdn_norm_gain,
           w_pool, pool_scale, w_out, norm_ffn_gain, w_gate_up, w_down, norm_final_gain):
    depth = w_ada.shape[0]
    assert depth == 1, "the block call applies the final norm, so only one layer is supported"
    batch = x.shape[0]
    l = 0
    mod = _modulation(c.astype(F32), w_ada[l], b_ada[l]).reshape(batch, N_MOD, D_MODEL)

    assert w_in.shape[-1] == IN_COLS
    w_main = w_in[l][:, :O_BA].astype(BF16)
    w_ba = jnp.pad(w_in[l][:, O_BA:IN_COLS],
                   ((0, 0), (0, LANES - 2 * HEADS))).astype(BF16)
    w_pool_bd = jnp.zeros((POOL_WIDTH, POOL_WIDTH), F32)
    for g in range(len(POOL_WINDOWS)):
        sl = slice(g * POOL_GROUP_DIM, (g + 1) * POOL_GROUP_DIM)
        w_pool_bd = w_pool_bd.at[sl, sl].set(w_pool[l, g])

    out = _decoder_block(
        x.astype(F32), mod, norm_mix_gain[l].reshape(1, D_MODEL).astype(F32),
        w_main, w_ba,
        conv_w[l].astype(F32),
        _lane_row(a_log[l], G_LANE),
        _lane_row(dt_bias[l], G_LANE),
        gdn_norm_gain[l].reshape(1, HEAD_DIM).astype(F32),
        w_pool_bd.astype(BF16), pool_scale[l].reshape(1, POOL_WIDTH).astype(F32),
        w_out[l].astype(BF16),
        norm_ffn_gain[l].reshape(1, D_MODEL).astype(F32),
        w_gate_up[l].astype(BF16),
        w_down[l].astype(BF16), norm_final_gain.reshape(1, D_MODEL).astype(F32))
    return out.astype(x.dtype)
```
